```python
import jax, jax.numpy as jnp
from jax import lax
import numpy as np

D_MODEL = 1024
BATCH = 4
SEQ = 4096
DEPTH = 4

N_MIXERS = 3
N_A = (DEPTH + 2) // 3
N_B = (DEPTH + 1) // 3
N_C = DEPTH // 3

N_HEADS = 16
HEAD_DIM = D_MODEL // N_HEADS
D_FF = ((8 * D_MODEL // 3 + 127) // 128) * 128
ROPE_THETA = 500000.0
ROT_DIM = HEAD_DIM // 4
Q_BLOCK = 128
EPS = 1e-6
NEG = -1e30

MOBA_BLOCK = 256
MOBA_TOPK = 3
MOBA_Q_CHUNK = 32

MLA_Q_RANK = 384
MLA_KV_RANK = 256
MLA_NOPE = 64
MLA_ROPE = 32
MLA_V = 64
MLA_IN = MLA_Q_RANK + MLA_KV_RANK + MLA_ROPE

kernel_name = "hybrid_sb_moba_mla_macaron"


def rms_norm(x, g):
    xf = x.astype(jnp.float32)
    y = xf * lax.rsqrt(jnp.mean(xf * xf, axis=-1, keepdims=True) + EPS)
    return (y * g.astype(jnp.float32)).astype(x.dtype)


def rotary(x, pos):
    r = x.shape[-1]
    inv = ROPE_THETA ** (-jnp.arange(0, r, 2, dtype=jnp.float32) / r)
    ang = pos.astype(jnp.float32)[:, None] * inv[None, :]
    cos, sin = jnp.cos(ang), jnp.sin(ang)
    xf = x.astype(jnp.float32)
    x1, x2 = xf[..., : r // 2], xf[..., r // 2:]
    return jnp.concatenate([x1 * cos - x2 * sin, x1 * sin + x2 * cos], axis=-1).astype(x.dtype)


def partial_rotary(x, pos):
    return jnp.concatenate([rotary(x[..., :ROT_DIM], pos), x[..., ROT_DIM:]], axis=-1)


def to_heads(t):
    b, s, _ = t.shape
    return t.reshape(b, s, N_HEADS, -1).transpose(0, 2, 1, 3)


def from_heads(t):
    b, h, s, d = t.shape
    return t.transpose(0, 2, 1, 3).reshape(b, s, h * d)


def merge_blocks(o):
    nb, b, h, q, d = o.shape
    return o.transpose(1, 2, 0, 3, 4).reshape(b, h, nb * q, d)


def swiglu(x, w_gate_up, w_down):
    g, u = jnp.split(x @ w_gate_up, 2, axis=-1)
    return (jax.nn.silu(g) * u) @ w_down


def stick_breaking_attention(x, w_qkv, w_o):
    b, s, _ = x.shape
    q, k, v = (to_heads(t) for t in jnp.split(x @ w_qkv, 3, axis=-1))
    scale = HEAD_DIM ** -0.5
    kpos = jnp.arange(s)

    def block(i):
        q0 = i * Q_BLOCK
        qb = lax.dynamic_slice_in_dim(q, q0, Q_BLOCK, axis=2)
        z = jnp.einsum('bhqd,bhkd->bhqk', qb, k).astype(jnp.float32) * scale
        qpos = q0 + jnp.arange(Q_BLOCK)
        past = kpos[None, :] < qpos[:, None]
        log_keep = jnp.where(past, -jax.nn.softplus(z), 0.0)
        tail = lax.cumsum(log_keep, axis=3, reverse=True) - log_keep
        w = jnp.where(past, jnp.exp(jax.nn.log_sigmoid(z) + tail), 0.0)
        return jnp.einsum('bhqk,bhkd->bhqd', w.astype(v.dtype), v)

    o = merge_blocks(lax.map(block, jnp.arange(s // Q_BLOCK)))
    return from_heads(o) @ w_o


def moba_attention(x, w_qkv, w_o):
    b, s, _ = x.shape
    pos = jnp.arange(s)
    q, k, v = (to_heads(t) for t in jnp.split(x @ w_qkv, 3, axis=-1))
    q, k = partial_rotary(q, pos), partial_rotary(k, pos)
    nb = -(-s // MOBA_BLOCK)
    s_pad = nb * MOBA_BLOCK
    pad = [(0, 0), (0, 0), (0, s_pad - s), (0, 0)]
    q, k, v = jnp.pad(q, pad), jnp.pad(k, pad), jnp.pad(v, pad)
    kb = k.reshape(b, N_HEADS, nb, MOBA_BLOCK, HEAD_DIM)
    vb = v.reshape(b, N_HEADS, nb, MOBA_BLOCK, HEAD_DIM)
    scale = HEAD_DIM ** -0.5

    k_mean = jnp.mean(kb.astype(jnp.float32), axis=3)
    gate = jnp.einsum('bhsd,bhnd->bhsn', q.astype(jnp.float32), k_mean)
    q_blk = jnp.arange(s_pad) // MOBA_BLOCK
    past_blk = jnp.arange(nb)[None, :] < q_blk[:, None]
    gate = jnp.where(past_blk, gate, NEG)
    topk = min(MOBA_TOPK, nb)
    _, sel = lax.top_k(gate, topk)
    sel_valid = sel < q_blk[:, None]

    bi = jnp.arange(b)[:, None, None, None]
    hi = jnp.arange(N_HEADS)[None, :, None, None]

    def chunk(c):
        q0 = c * MOBA_Q_CHUNK
        qc = lax.dynamic_slice_in_dim(q, q0, MOBA_Q_CHUNK, axis=2)
        idx = lax.dynamic_slice_in_dim(sel, q0, MOBA_Q_CHUNK, axis=2)
        valid = lax.dynamic_slice_in_dim(sel_valid, q0, MOBA_Q_CHUNK, axis=2)
        kg = kb[bi, hi, idx]
        vg = vb[bi, hi, idx]
        l_sel = jnp.einsum('bhqd,bhqjkd->bhqjk', qc, kg).astype(jnp.float32) * scale
        l_sel = jnp.where(valid[..., None], l_sel, NEG)
        l_sel = l_sel.reshape(b, N_HEADS, MOBA_Q_CHUNK, topk * MOBA_BLOCK)
        own = q0 // MOBA_BLOCK
        k_own = lax.dynamic_index_in_dim(kb, own, axis=2, keepdims=False)
        v_own = lax.dynamic_index_in_dim(vb, own, axis=2, keepdims=False)
        l_own = jnp.einsum('bhqd,bhkd->bhqk', qc, k_own).astype(jnp.float32) * scale
        cpos = q0 + jnp.arange(MOBA_Q_CHUNK)
        opos = own * MOBA_BLOCK + jnp.arange(MOBA_BLOCK)
        l_own = jnp.where(opos[None, :] <= cpos[:, None], l_own, NEG)
        p = jax.nn.softmax(jnp.concatenate([l_sel, l_own], axis=-1), axis=-1).astype(v.dtype)
        p_sel = p[..., : topk * MOBA_BLOCK].reshape(b, N_HEADS, MOBA_Q_CHUNK, topk, MOBA_BLOCK)
        p_own = p[..., topk * MOBA_BLOCK:]
        return (jnp.einsum('bhqjk,bhqjkd->bhqd', p_sel, vg)
                + jnp.einsum('bhqk,bhkd->bhqd', p_own, v_own))

    o = merge_blocks(lax.map(chunk, jnp.arange(s_pad // MOBA_Q_CHUNK)))[:, :, :s]
    return from_heads(o) @ w_o


def mla_attention(x, w_in, q_norm, w_uq, kv_norm, w_ukv, w_o):
    b, s, _ = x.shape
    pos = jnp.arange(s)
    cq, ckv, kr = jnp.split(x @ w_in, [MLA_Q_RANK, MLA_Q_RANK + MLA_KV_RANK], axis=-1)
    q = to_heads(rms_norm(cq, q_norm) @ w_uq)
    q_nope, q_rope = q[..., :MLA_NOPE], rotary(q[..., MLA_NOPE:], pos)
    kv = to_heads(rms_norm(ckv, kv_norm) @ w_ukv)
    k_nope, v = kv[..., :MLA_NOPE], kv[..., MLA_NOPE:]
    k_rope = rotary(kr, pos)
    scale = (MLA_NOPE + MLA_ROPE) ** -0.5
    kpos = jnp.arange(s)

    def block(i):
        q0 = i * Q_BLOCK
        qn = lax.dynamic_slice_in_dim(q_nope, q0, Q_BLOCK, axis=2)
        qr = lax.dynamic_slice_in_dim(q_rope, q0, Q_BLOCK, axis=2)
        logits = (jnp.einsum('bhqd,bhkd->bhqk', qn, k_nope)
                  + jnp.einsum('bhqr,bkr->bhqk', qr, k_rope)).astype(jnp.float32) * scale
        qpos = q0 + jnp.arange(Q_BLOCK)
        logits = jnp.where(kpos[None, :] <= qpos[:, None], logits, NEG)
        p = jax.nn.softmax(logits, axis=-1).astype(v.dtype)
        return jnp.einsum('bhqk,bhkd->bhqd', p, v)

    o = merge_blocks(lax.map(block, jnp.arange(s // Q_BLOCK)))
    return from_heads(o) @ w_o


def setup_inputs(seed: int = 0) -> dict:
    key = jax.random.key(seed)
    ks = iter(jax.random.split(key, 32))

    def w(shape, fan_in):
        return jax.random.normal(next(ks), shape, jnp.float32) * (fan_in ** -0.5)

    def gain(shape):
        return 1.0 + 0.02 * jax.random.normal(next(ks), shape, jnp.float32)

    D = D_MODEL
    return {
        "x": jax.random.normal(next(ks), (BATCH, SEQ, D), jnp.float32),
        "norm_ffn1": gain((DEPTH, D)),
        "ffn1_w_gate_up": w((DEPTH, D, 2 * D_FF), D),
        "ffn1_w_down": w((DEPTH, D_FF, D), D_FF),
        "norm_mix": gain((DEPTH, D)),
        "norm_ffn2": gain((DEPTH, D)),
        "ffn2_w_gate_up": w((DEPTH, D, 2 * D_FF), D),
        "ffn2_w_down": w((DEPTH, D_FF, D), D_FF),
        "sb_w_qkv": w((N_A, D, 3 * D), D),
        "sb_w_o": w((N_A, D, D), D),
        "moba_w_qkv": w((N_B, D, 3 * D), D),
        "moba_w_o": w((N_B, D, D), D),
        "mla_w_in": w((N_C, D, MLA_IN), D),
        "mla_q_norm": gain((N_C, MLA_Q_RANK)),
        "mla_w_uq": w((N_C, MLA_Q_RANK, N_HEADS * (MLA_NOPE + MLA_ROPE)), MLA_Q_RANK),
        "mla_kv_norm": gain((N_C, MLA_KV_RANK)),
        "mla_w_ukv": w((N_C, MLA_KV_RANK, N_HEADS * (MLA_NOPE + MLA_V)), MLA_KV_RANK),
        "mla_w_o": w((N_C, N_HEADS * MLA_V, D), N_HEADS * MLA_V),
        "final_norm": gain((D,)),
    }


def reference(x, norm_ffn1, ffn1_w_gate_up, ffn1_w_down, norm_mix, norm_ffn2,
              ffn2_w_gate_up, ffn2_w_down, sb_w_qkv, sb_w_o, moba_w_qkv, moba_w_o,
              mla_w_in, mla_q_norm, mla_w_uq, mla_kv_norm, mla_w_ukv, mla_w_o,
              final_norm):
    h = x
    for i in range(DEPTH):
        h = h + 0.5 * swiglu(rms_norm(h, norm_ffn1[i]), ffn1_w_gate_up[i], ffn1_w_down[i])
        hn = rms_norm(h, norm_mix[i])
        kind, j = i % N_MIXERS, i // N_MIXERS
        if kind == 0:
            mix = stick_breaking_attention(hn, sb_w_qkv[j], sb_w_o[j])
        elif kind == 1:
            mix = moba_attention(hn, moba_w_qkv[j], moba_w_o[j])
        else:
            mix = mla_attention(hn, mla_w_in[j], mla_q_norm[j], mla_w_uq[j],
                                mla_kv_norm[j], mla_w_ukv[j], mla_w_o[j])
        h = h + mix
        h = h + 0.5 * swiglu(rms_norm(h, norm_ffn2[i]), ffn2_w_gate_up[i], ffn2_w_down[i])
    return rms_norm(h, final_norm)
```

```python
import functools

import jax
import jax.numpy as jnp
from jax import lax
from jax.experimental import pallas as pl
from jax.experimental.pallas import tpu as pltpu

F32 = jnp.float32
BF16 = jnp.bfloat16

DEPTH = 4
N_HEADS = 16
HEAD_DIM = 64
ROPE_THETA = 500000.0
ROT_DIM = HEAD_DIM // 4
EPS = 1e-6
NEG = -1e30

MOBA_BLOCK = 256
MOBA_TOPK = 3

MLA_Q_RANK = 384
MLA_KV_RANK = 256
MLA_NOPE = 64
MLA_ROPE = 32
MLA_V = 64

ATTN_TILE = 256
VMEM_LIMIT = 48 * 1024 * 1024

_NT = (((1,), (1,)), ((), ()))


def _cparams(*sem):
    return pltpu.CompilerParams(dimension_semantics=sem, vmem_limit_bytes=VMEM_LIMIT)


def _rms(x, g):
    ms = jnp.mean(x * x, axis=-1, keepdims=True)
    return x * lax.rsqrt(ms + EPS) * g


def _dot(a, b):
    return jnp.dot(a, b, preferred_element_type=F32)


def _ffn_kernel(h_ref, g_ref, wg_ref, wu_ref, wd_ref, gf_ref, o_ref, xn_ref, acc_ref, *, nk, final):
    k = pl.program_id(1)

    @pl.when(k == 0)
    def _():
        xn_ref[...] = _rms(h_ref[...], g_ref[...]).astype(BF16)
        acc_ref[...] = jnp.zeros_like(acc_ref)

    xn = xn_ref[...]
    gate = _dot(xn, wg_ref[...])
    up = _dot(xn, wu_ref[...])
    act = (gate * jax.nn.sigmoid(gate) * up).astype(BF16)
    acc_ref[...] += _dot(act, wd_ref[...])

    @pl.when(k == nk - 1)
    def _():
        y = h_ref[...] + 0.5 * acc_ref[...]
        if final:
            y = _rms(y, gf_ref[...])
        o_ref[...] = y


def _ffn(h, g, w_gate_up, w_down, g_final, *, final, tm=512, tf=1408):
    n, d = h.shape
    dff = w_down.shape[0]
    nk = dff // tf
    assert nk * tf == dff and n % tm == 0
    return pl.pallas_call(
        functools.partial(_ffn_kernel, nk=nk, final=final),
        grid=(n // tm, nk),
        in_specs=[
            pl.BlockSpec((tm, d), lambda i, k: (i, 0)),
            pl.BlockSpec((1, d), lambda i, k: (0, 0)),
            pl.BlockSpec((d, tf), lambda i, k: (0, k)),
            pl.BlockSpec((d, tf), lambda i, k: (0, k + nk)),
            pl.BlockSpec((tf, d), lambda i, k: (k, 0)),
            pl.BlockSpec((1, d), lambda i, k: (0, 0)),
        ],
        out_specs=pl.BlockSpec((tm, d), lambda i, k: (i, 0)),
        out_shape=jax.ShapeDtypeStruct((n, d), F32),
        scratch_shapes=[pltpu.VMEM((tm, d), BF16), pltpu.VMEM((tm, d), F32)],
        compiler_params=_cparams("parallel", "arbitrary"),
        name="ffn_final" if final else "ffn",
    )(h, g.reshape(1, d), w_gate_up, w_gate_up, w_down, g_final.reshape(1, d))


def _oproj_kernel(h_ref, o_ref, w_ref, out_ref):
    out_ref[...] = h_ref[...] + _dot(o_ref[...], w_ref[...])


def _oproj(h, o, w, *, tm=512):
    n, d = h.shape
    return pl.pallas_call(
        _oproj_kernel,
        grid=(n // tm,),
        in_specs=[
            pl.BlockSpec((tm, d), lambda i: (i, 0)),
            pl.BlockSpec((tm, o.shape[1]), lambda i: (i, 0)),
            pl.BlockSpec(w.shape, lambda i: (0, 0)),
        ],
        out_specs=pl.BlockSpec((tm, d), lambda i: (i, 0)),
        out_shape=jax.ShapeDtypeStruct((n, d), F32),
        compiler_params=_cparams("parallel"),
        name="oproj",
    )(h, o, w)


def _to_heads(t, b, s):
    return t.reshape(b, s, N_HEADS, -1).transpose(0, 2, 1, 3)


def _from_heads(t):
    b, h, s, d = t.shape
    return t.transpose(0, 2, 1, 3).reshape(b * s, h * d)


def _sb_proj_kernel(h_ref, g_ref, w_ref, q_ref, k_ref, v_ref, *, scale):
    d = h_ref.shape[-1]
    xn = _rms(h_ref[...], g_ref[...]).astype(BF16)
    y = _dot(xn, w_ref[...])
    q_ref[...] = (y[:, :d] * scale).astype(BF16)
    k_ref[...] = y[:, d:2 * d].astype(BF16)
    v_ref[...] = y[:, 2 * d:].astype(BF16)


def _sb_proj(h, g, w, *, tm=512):
    n, d = h.shape
    out = jax.ShapeDtypeStruct((n, d), BF16)
    return pl.pallas_call(
        functools.partial(_sb_proj_kernel, scale=HEAD_DIM ** -0.5),
        grid=(n // tm,),
        in_specs=[
            pl.BlockSpec((tm, d), lambda i: (i, 0)),
            pl.BlockSpec((1, d), lambda i: (0, 0)),
            pl.BlockSpec(w.shape, lambda i: (0, 0)),
        ],
        out_specs=[pl.BlockSpec((tm, d), lambda i: (i, 0))] * 3,
        out_shape=[out, out, out],
        compiler_params=_cparams("parallel"),
        name="sb_proj",
    )(h, g.reshape(1, d), w)


def _sb_attn_kernel(q_ref, k_ref, v_ref, o_ref, *, t):
    i = pl.program_id(1)
    q = q_ref[0]
    row = lax.broadcasted_iota(jnp.int32, (t, t), 0)
    col = lax.broadcasted_iota(jnp.int32, (t, t), 1)
    tri = jnp.where(row > col, 1.0, 0.0).astype(BF16)
    past = col < row

    def tile(j, c, o, diag):
        ks = pl.multiple_of(j * t, t)
        kb = k_ref[0, pl.ds(ks, t), :]
        vb = v_ref[0, pl.ds(ks, t), :]
        z = lax.dot_general(q, kb, _NT, preferred_element_type=F32)
        sp = jnp.log(1.0 + jnp.exp(-jnp.abs(z)))
        log_keep = -(jnp.maximum(z, 0.0) + sp)
        log_beta = jnp.minimum(z, 0.0) - sp
        if diag:
            log_keep = jnp.where(past, log_keep, 0.0)
        hi = log_keep.astype(BF16)
        lo = (log_keep - hi.astype(F32)).astype(BF16)
        tail = _dot(hi, tri) + _dot(lo, tri)
        w = jnp.exp(log_beta + tail + c)
        if diag:
            w = jnp.where(past, w, 0.0)
        o = o + _dot(w.astype(BF16), vb)
        c = c + jnp.sum(log_keep, axis=-1, keepdims=True)
        return c, o

    c = jnp.zeros((t, 1), F32)
    o = jnp.zeros((t, HEAD_DIM), F32)
    c, o = tile(i, c, o, True)
    c, o = lax.fori_loop(0, i, lambda s, co: tile(i - 1 - s, co[0], co[1], False), (c, o))
    o_ref[0] = o.astype(o_ref.dtype)


def _sb_attn(q, k, v):
    bh, s, dh = q.shape
    t = ATTN_TILE
    return pl.pallas_call(
        functools.partial(_sb_attn_kernel, t=t),
        grid=(bh, s // t),
        in_specs=[
            pl.BlockSpec((1, t, dh), lambda b, i: (b, i, 0)),
            pl.BlockSpec((1, s, dh), lambda b, i: (b, 0, 0)),
            pl.BlockSpec((1, s, dh), lambda b, i: (b, 0, 0)),
        ],
        out_specs=pl.BlockSpec((1, t, dh), lambda b, i: (b, i, 0)),
        out_shape=jax.ShapeDtypeStruct((bh, s, dh), BF16),
        compiler_params=_cparams("parallel", "arbitrary"),
        name="sb_attn",
    )(q, k, v)


def _sb_mixer(h, g, w_qkv, w_o, b, s):
    q, k, v = _sb_proj(h, g, w_qkv)
    q, k, v = (_to_heads(x, b, s).reshape(b * N_HEADS, s, HEAD_DIM) for x in (q, k, v))
    o = _sb_attn(q, k, v).reshape(b, N_HEADS, s, HEAD_DIM)
    return _oproj(h, _from_heads(o), w_o)


def _softmax_tile(sc, m, l, acc, vb):
    m_new = jnp.maximum(m, jnp.max(sc, axis=-1, keepdims=True))
    alpha = jnp.exp(m - m_new)
    p = jnp.exp(sc - m_new)
    l = alpha * l + jnp.sum(p, axis=-1, keepdims=True)
    acc = alpha * acc + _dot(p.astype(BF16), vb)
    return m_new, l, acc


def _rot_tables(s, r, width):
    inv = ROPE_THETA ** (-jnp.arange(0, r, 2, dtype=F32) / r)
    ang = jnp.arange(s).astype(F32)[:, None] * inv[None, :]
    cos, sin = jnp.cos(ang), jnp.sin(ang)
    pad1 = jnp.ones((s, width - r), F32)
    pad0 = jnp.zeros((s, width - r), F32)
    return (jnp.concatenate([cos, cos, pad1], axis=-1),
            jnp.concatenate([-sin, sin, pad0], axis=-1))


def _swap_perm(r, width, n):
    d = jnp.arange(width)
    p = jnp.where(d < r // 2, d + r // 2, jnp.where(d < r, d - r // 2, d))
    return (jnp.arange(n)[:, None] * width + p[None, :]).reshape(-1)


def _moba_proj_kernel(h_ref, g_ref, w_ref, cos_ref, sin_ref, qs_ref, qf_ref, k_ref, v_ref, km_ref, *, scale):
    d = h_ref.shape[-1]
    xn = _rms(h_ref[0], g_ref[...]).astype(BF16)
    y = _dot(xn, w_ref[...])
    cos, sin = cos_ref[...], sin_ref[...]
    q = y[:, :d] * cos + y[:, d:2 * d] * sin
    k = y[:, 2 * d:3 * d] * cos + y[:, 3 * d:4 * d] * sin
    qs_ref[0] = (q * scale).astype(BF16)
    qf_ref[0] = q
    k_ref[0] = k.astype(BF16)
    v_ref[0] = y[:, 4 * d:].astype(BF16)
    km_ref[0, 0] = jnp.mean(k, axis=0, keepdims=True)


def _moba_proj(h3, g, w_ext, cos, sin):
    b, s, d = h3.shape
    tm = MOBA_BLOCK
    nb = s // tm
    tok = lambda dt: jax.ShapeDtypeStruct((b, s, d), dt)
    tok_spec = pl.BlockSpec((1, tm, d), lambda bi, i: (bi, i, 0))
    return pl.pallas_call(
        functools.partial(_moba_proj_kernel, scale=HEAD_DIM ** -0.5),
        grid=(b, nb),
        in_specs=[
            tok_spec,
            pl.BlockSpec((1, d), lambda bi, i: (0, 0)),
            pl.BlockSpec(w_ext.shape, lambda bi, i: (0, 0)),
            pl.BlockSpec((tm, d), lambda bi, i: (i, 0)),
            pl.BlockSpec((tm, d), lambda bi, i: (i, 0)),
        ],
        out_specs=[tok_spec, tok_spec, tok_spec, tok_spec,
                   pl.BlockSpec((1, 1, 1, d), lambda bi, i: (bi, i, 0, 0))],
        out_shape=[tok(BF16), tok(F32), tok(BF16), tok(BF16),
                   jax.ShapeDtypeStruct((b, nb, 1, d), F32)],
        compiler_params=_cparams("parallel", "parallel"),
        name="moba_proj",
    )(h3, g.reshape(1, d), w_ext, cos, sin)


def _moba_select_kernel(q_ref, km_ref, sel_ref, *, ts, nb):
    t0 = pl.program_id(1) * ts
    q = q_ref[0]
    km = km_ref[0]
    r = lax.broadcasted_iota(jnp.int32, km.shape, 0)
    c = lax.broadcasted_iota(jnp.int32, km.shape, 1)
    km = jnp.where((r % N_HEADS) == (c // HEAD_DIM), km, 0.0)
    qh = q.astype(BF16)
    ql = (q - qh.astype(F32)).astype(BF16)
    kh = km.astype(BF16)
    kl = (km - kh.astype(F32)).astype(BF16)
    dg = lambda a, bb: lax.dot_general(a, bb, _NT, preferred_element_type=F32)
    gate = dg(kh, qh) + dg(kh, ql) + dg(kl, qh)
    qblk = (t0 + lax.broadcasted_iota(jnp.int32, (N_HEADS, ts), 1)) // MOBA_BLOCK
    gs = [jnp.where(n < qblk, gate[n * N_HEADS:(n + 1) * N_HEADS, :], NEG) for n in range(nb)]
    for n in range(nb):
        rank = jnp.zeros((N_HEADS, ts), F32)
        for m in range(nb):
            if m == n:
                continue
            beats = (gs[m] >= gs[n]) if m < n else (gs[m] > gs[n])
            rank = rank + jnp.where(beats, 1.0, 0.0)
        keep = jnp.where(rank < MOBA_TOPK - 0.5, 1.0, 0.0)
        sel_ref[0, n] = jnp.where(n < qblk, keep, 0.0)


def _moba_select(qf, km_rep, *, ts=512):
    b, s, d = qf.shape
    nb = s // MOBA_BLOCK
    return pl.pallas_call(
        functools.partial(_moba_select_kernel, ts=ts, nb=nb),
        grid=(b, s // ts),
        in_specs=[
            pl.BlockSpec((1, ts, d), lambda bi, i: (bi, i, 0)),
            pl.BlockSpec((1, nb * N_HEADS, d), lambda bi, i: (bi, 0, 0)),
        ],
        out_specs=pl.BlockSpec((1, nb, N_HEADS, ts), lambda bi, i: (bi, 0, 0, i)),
        out_shape=jax.ShapeDtypeStruct((b, nb, N_HEADS, s), F32),
        compiler_params=_cparams("parallel", "parallel"),
        name="moba_select",
    )(qf, km_rep)


def _moba_attn_kernel(q_ref, k_ref, v_ref, sel_ref, o_ref, *, t):
    i = pl.program_id(1)
    q = q_ref[0]
    sel = sel_ref[0]
    lane = lax.broadcasted_iota(jnp.int32, sel.shape, 1)
    row = lax.broadcasted_iota(jnp.int32, (t, t), 0)
    col = lax.broadcasted_iota(jnp.int32, (t, t), 1)

    def scores(n):
        ks = pl.multiple_of(n * t, t)
        kb = k_ref[0, pl.ds(ks, t), :]
        vb = v_ref[0, pl.ds(ks, t), :]
        return lax.dot_general(q, kb, _NT, preferred_element_type=F32), vb

    sc, vb = scores(i)
    sc = jnp.where(col <= row, sc, NEG)
    m = jnp.full((t, 1), NEG, F32)
    l = jnp.zeros((t, 1), F32)
    acc = jnp.zeros((t, HEAD_DIM), F32)
    m, l, acc = _softmax_tile(sc, m, l, acc, vb)

    def body(n, carry):
        sc, vb = scores(n)
        chosen = jnp.sum(jnp.where(lane == n, sel, 0.0), axis=-1, keepdims=True)
        sc = jnp.where(chosen > 0.5, sc, NEG)
        return _softmax_tile(sc, *carry, vb)

    m, l, acc = lax.fori_loop(0, i, body, (m, l, acc))
    o_ref[0] = (acc / l).astype(o_ref.dtype)


def _moba_attn(q, k, v, sel):
    bh, s, dh = q.shape
    t = MOBA_BLOCK
    nb = s // t
    return pl.pallas_call(
        functools.partial(_moba_attn_kernel, t=t),
        grid=(bh, nb),
        in_specs=[
            pl.BlockSpec((1, t, dh), lambda b, i: (b, i, 0)),
            pl.BlockSpec((1, s, dh), lambda b, i: (b, 0, 0)),
            pl.BlockSpec((1, s, dh), lambda b, i: (b, 0, 0)),
            pl.BlockSpec((1, t, nb), lambda b, i: (b, i, 0)),
        ],
        out_specs=pl.BlockSpec((1, t, dh), lambda b, i: (b, i, 0)),
        out_shape=jax.ShapeDtypeStruct((bh, s, dh), BF16),
        compiler_params=_cparams("parallel", "arbitrary"),
        name="moba_attn",
    )(q, k, v, sel)


def _moba_mixer(h, g, w_qkv, w_o, b, s):
    n, d = h.shape
    assert s % MOBA_BLOCK == 0
    nb = s // MOBA_BLOCK
    perm = _swap_perm(ROT_DIM, HEAD_DIM, N_HEADS)
    wq, wk, wv = w_qkv[:, :d], w_qkv[:, d:2 * d], w_qkv[:, 2 * d:]
    w_ext = jnp.concatenate([wq, wq[:, perm], wk, wk[:, perm], wv], axis=1).astype(BF16)
    cos, sin = _rot_tables(s, ROT_DIM, HEAD_DIM)
    cos, sin = jnp.tile(cos, (1, N_HEADS)), jnp.tile(sin, (1, N_HEADS))
    qs, qf, k, v, km = _moba_proj(h.reshape(b, s, d), g, w_ext, cos, sin)
    km_rep = jnp.repeat(km.reshape(b, nb, d), N_HEADS, axis=1)
    sel = _moba_select(qf, km_rep)
    sel = sel.transpose(0, 2, 3, 1).reshape(b * N_HEADS, s, nb)
    qs, k, v = (_to_heads(x, b, s).reshape(b * N_HEADS, s, HEAD_DIM) for x in (qs, k, v))
    o = _moba_attn(qs, k, v, sel).reshape(b, N_HEADS, s, HEAD_DIM)
    return _oproj(h, _from_heads(o), w_o)


def _mla_proj_kernel(h_ref, g_ref, win_ref, gq_ref, gkv_ref, wuq_ref, wukv_ref,
                     cq_ref, sq_ref, ck_ref, sk_ref,
                     qn_ref, qr_ref, kn_ref, v_ref, kr_ref, *, scale):
    hd = N_HEADS * MLA_NOPE
    hr = N_HEADS * MLA_ROPE
    xn = _rms(h_ref[0], g_ref[...]).astype(BF16)
    c = _dot(xn, win_ref[...])
    a0, a1, a2 = MLA_Q_RANK, MLA_Q_RANK + MLA_KV_RANK, MLA_Q_RANK + MLA_KV_RANK + MLA_ROPE
    cq = _rms(c[:, :a0], gq_ref[...]).astype(BF16)
    ckv = _rms(c[:, a0:a1], gkv_ref[...]).astype(BF16)
    kr = c[:, a1:a2] * ck_ref[...] + c[:, a2:a2 + MLA_ROPE] * sk_ref[...]
    q = _dot(cq, wuq_ref[...])
    kv = _dot(ckv, wukv_ref[...])
    qr = q[:, hd:hd + hr] * cq_ref[...] + q[:, hd + hr:] * sq_ref[...]
    qn_ref[0] = (q[:, :hd] * scale).astype(BF16)
    qr_ref[0] = (qr * scale).astype(BF16)
    kn_ref[0] = kv[:, :hd].astype(BF16)
    v_ref[0] = kv[:, hd:].astype(BF16)
    kr_ref[0] = kr.astype(BF16)


def _mla_proj(h3, g, w_in_ext, gq, gkv, w_uq_ext, w_ukv_ext, cq, sq, ck, sk, *, tm=512):
    b, s, d = h3.shape
    hd, hr = N_HEADS * MLA_NOPE, N_HEADS * MLA_ROPE
    hv = N_HEADS * MLA_V
    full = lambda a: pl.BlockSpec(a.shape, lambda bi, i: (0,) * a.ndim)
    tok = lambda w: pl.BlockSpec((1, tm, w), lambda bi, i: (bi, i, 0))
    tab = lambda w: pl.BlockSpec((tm, w), lambda bi, i: (i, 0))
    gq2, gkv2, g2 = gq.reshape(1, -1), gkv.reshape(1, -1), g.reshape(1, d)
    return pl.pallas_call(
        functools.partial(_mla_proj_kernel, scale=(MLA_NOPE + MLA_ROPE) ** -0.5),
        grid=(b, s // tm),
        in_specs=[tok(d), full(g2), full(w_in_ext), full(gq2), full(gkv2), full(w_uq_ext), full(w_ukv_ext),
                  tab(hr), tab(hr), tab(MLA_ROPE), tab(MLA_ROPE)],
        out_specs=[tok(hd), tok(hr), tok(hd), tok(hv), tok(MLA_ROPE)],
        out_shape=[jax.ShapeDtypeStruct((b, s, w), BF16) for w in (hd, hr, hd, hv, MLA_ROPE)],
        compiler_params=_cparams("parallel", "parallel"),
        name="mla_proj",
    )(h3, g2, w_in_ext, gq2, gkv2, w_uq_ext, w_ukv_ext, cq, sq, ck, sk)


def _mla_attn_kernel(qn_ref, qr_ref, kn_ref, kr_ref, v_ref, o_ref, *, t):
    i = pl.program_id(2)
    qn = qn_ref[0, 0]
    qr = qr_ref[0, 0]
    row = lax.broadcasted_iota(jnp.int32, (t, t), 0)
    col = lax.broadcasted_iota(jnp.int32, (t, t), 1)

    def scores(j):
        ks = pl.multiple_of(j * t, t)
        kn = kn_ref[0, 0, pl.ds(ks, t), :]
        kr = kr_ref[0, pl.ds(ks, t), :]
        vb = v_ref[0, 0, pl.ds(ks, t), :]
        sc = (lax.dot_general(qn, kn, _NT, preferred_element_type=F32)
              + lax.dot_general(qr, kr, _NT, preferred_element_type=F32))
        return sc, vb

    sc, vb = scores(i)
    sc = jnp.where(col <= row, sc, NEG)
    m = jnp.full((t, 1), NEG, F32)
    l = jnp.zeros((t, 1), F32)
    acc = jnp.zeros((t, MLA_V), F32)
    m, l, acc = _softmax_tile(sc, m, l, acc, vb)

    def body(j, carry):
        sc, vb = scores(j)
        return _softmax_tile(sc, *carry, vb)

    m, l, acc = lax.fori_loop(0, i, body, (m, l, acc))
    o_ref[0, 0] = (acc / l).astype(o_ref.dtype)


def _mla_attn(qn, qr, kn, kr, v):
    b, nh, s, _ = qn.shape
    t = ATTN_TILE
    qspec = lambda w: pl.BlockSpec((1, 1, t, w), lambda bi, hi, i: (bi, hi, i, 0))
    kspec = lambda w: pl.BlockSpec((1, 1, s, w), lambda bi, hi, i: (bi, hi, 0, 0))
    return pl.pallas_call(
        functools.partial(_mla_attn_kernel, t=t),
        grid=(b, nh, s // t),
        in_specs=[qspec(MLA_NOPE), qspec(MLA_ROPE), kspec(MLA_NOPE),
                  pl.BlockSpec((1, s, MLA_ROPE), lambda bi, hi, i: (bi, 0, 0)),
                  kspec(MLA_V)],
        out_specs=qspec(MLA_V),
        out_shape=jax.ShapeDtypeStruct((b, nh, s, MLA_V), BF16),
        compiler_params=_cparams("parallel", "parallel", "arbitrary"),
        name="mla_attn",
    )(qn, qr, kn, kr, v)


def _mla_mixer(h, g, w_in, q_norm, w_uq, kv_norm, w_ukv, w_o, b, s):
    n, d = h.shape
    a1 = MLA_Q_RANK + MLA_KV_RANK
    w_kr = w_in[:, a1:]
    w_in_ext = jnp.concatenate(
        [w_in, w_kr[:, _swap_perm(MLA_ROPE, MLA_ROPE, 1)],
         jnp.zeros((d, 768 - a1 - 2 * MLA_ROPE), w_in.dtype)], axis=1).astype(BF16)
    wq = w_uq.reshape(MLA_Q_RANK, N_HEADS, MLA_NOPE + MLA_ROPE)
    wq_n = wq[:, :, :MLA_NOPE].reshape(MLA_Q_RANK, -1)
    wq_r = wq[:, :, MLA_NOPE:].reshape(MLA_Q_RANK, -1)
    w_uq_ext = jnp.concatenate(
        [wq_n, wq_r, wq_r[:, _swap_perm(MLA_ROPE, MLA_ROPE, N_HEADS)]], axis=1).astype(BF16)
    wkv = w_ukv.reshape(MLA_KV_RANK, N_HEADS, MLA_NOPE + MLA_V)
    w_ukv_ext = jnp.concatenate(
        [wkv[:, :, :MLA_NOPE].reshape(MLA_KV_RANK, -1), wkv[:, :, MLA_NOPE:].reshape(MLA_KV_RANK, -1)],
        axis=1).astype(BF16)
    ck, sk = _rot_tables(s, MLA_ROPE, MLA_ROPE)
    cq, sq = jnp.tile(ck, (1, N_HEADS)), jnp.tile(sk, (1, N_HEADS))
    qn, qr, kn, v, kr = _mla_proj(h.reshape(b, s, d), g, w_in_ext, q_norm, kv_norm,
                                  w_uq_ext, w_ukv_ext, cq, sq, ck, sk)
    qn, qr, kn, v = (_to_heads(x, b, s) for x in (qn, qr, kn, v))
    o = _mla_attn(qn, qr, kn, kr, v)
    return _oproj(h, _from_heads(o), w_o)


def kernel(x, norm_ffn1, ffn1_w_gate_up, ffn1_w_down, norm_mix, norm_ffn2, ffn2_w_gate_up, ffn2_w_down,
           sb_w_qkv, sb_w_o, moba_w_qkv, moba_w_o, mla_w_in, mla_q_norm, mla_w_uq, mla_kv_norm,
           mla_w_ukv, mla_w_o, final_norm):
    b, s, d = x.shape
    h = x.reshape(b * s, d)
    bf = lambda w: w.astype(BF16)
    for i in range(DEPTH):
        h = _ffn(h, norm_ffn1[i], bf(ffn1_w_gate_up[i]), bf(ffn1_w_down[i]), final_norm, final=False)
        kind, j = i % 3, i // 3
        if kind == 0:
            h = _sb_mixer(h, norm_mix[i], bf(sb_w_qkv[j]), bf(sb_w_o[j]), b, s)
        elif kind == 1:
            h = _moba_mixer(h, norm_mix[i], moba_w_qkv[j], bf(moba_w_o[j]), b, s)
        else:
            h = _mla_mixer(h, norm_mix[i], mla_w_in[j], mla_q_norm[j], mla_w_uq[j], mla_kv_norm[j],
                           mla_w_ukv[j], bf(mla_w_o[j]), b, s)
        h = _ffn(h, norm_ffn2[i], bf(ffn2_w_gate_up[i]), bf(ffn2_w_down[i]), final_norm,
                 final=(i == DEPTH - 1))
    return h.reshape(b, s, d)
```

```python
import functools

import jax
import jax.numpy as jnp
from jax import lax
from jax.experimental import pallas as pl
from jax.experimental.pallas import tpu as pltpu

F32 = jnp.float32
BF16 = jnp.bfloat16

DEPTH = 4
N_HEADS = 16
HEAD_DIM = 64
ROPE_THETA = 500000.0
ROT_DIM = HEAD_DIM // 4
EPS = 1e-6
NEG = -1e30
LOG2E = 1.4426950408889634

MOBA_BLOCK = 256
MOBA_TOPK = 3

MLA_Q_RANK = 384
MLA_KV_RANK = 256
MLA_NOPE = 64
MLA_ROPE = 32
MLA_V = 64

LANES = 128
ATTN_TILE = 256
HEADS_PER_STEP = 8
VMEM_LIMIT = 48 * 1024 * 1024

_NT = (((1,), (1,)), ((), ()))


def _cparams(*sem):
    return pltpu.CompilerParams(dimension_semantics=sem, vmem_limit_bytes=VMEM_LIMIT)


def _rms(x, g):
    ms = jnp.mean(x * x, axis=-1, keepdims=True)
    return x * lax.rsqrt(ms + EPS) * g


def _dot(a, b):
    return jnp.dot(a, b, preferred_element_type=F32)


def _dot_nt(a, b):
    return lax.dot_general(a, b, _NT, preferred_element_type=F32)


def _ffn_kernel(h_ref, g_ref, wg_ref, wu_ref, wd_ref, gf_ref, o_ref, xn_ref, acc_ref, *, nk, final):
    k = pl.program_id(1)

    @pl.when(k == 0)
    def _():
        xn_ref[...] = _rms(h_ref[...], g_ref[...]).astype(BF16)
        acc_ref[...] = jnp.zeros_like(acc_ref)

    xn = xn_ref[...]
    gate = _dot(xn, wg_ref[...])
    up = _dot(xn, wu_ref[...])
    act = (gate * jax.nn.sigmoid(gate) * up).astype(BF16)
    acc_ref[...] += _dot(act, wd_ref[...])

    @pl.when(k == nk - 1)
    def _():
        y = h_ref[...] + 0.5 * acc_ref[...]
        if final:
            y = _rms(y, gf_ref[...])
        o_ref[...] = y


def _ffn(h, g, w_gate_up, w_down, g_final, *, final, tm=512, tf=1408):
    n, d = h.shape
    dff = w_down.shape[0]
    nk = dff // tf
    assert nk * tf == dff and n % tm == 0
    return pl.pallas_call(
        functools.partial(_ffn_kernel, nk=nk, final=final),
        grid=(n // tm, nk),
        in_specs=[
            pl.BlockSpec((tm, d), lambda i, k: (i, 0)),
            pl.BlockSpec((1, d), lambda i, k: (0, 0)),
            pl.BlockSpec((d, tf), lambda i, k: (0, k)),
            pl.BlockSpec((d, tf), lambda i, k: (0, k + nk)),
            pl.BlockSpec((tf, d), lambda i, k: (k, 0)),
            pl.BlockSpec((1, d), lambda i, k: (0, 0)),
        ],
        out_specs=pl.BlockSpec((tm, d), lambda i, k: (i, 0)),
        out_shape=jax.ShapeDtypeStruct((n, d), F32),
        scratch_shapes=[pltpu.VMEM((tm, d), BF16), pltpu.VMEM((tm, d), F32)],
        compiler_params=_cparams("parallel", "arbitrary"),
        name="ffn_final" if final else "ffn",
    )(h, g.reshape(1, d), w_gate_up, w_gate_up, w_down, g_final.reshape(1, d))


def _oproj_kernel(h_ref, o_ref, w_ref, out_ref):
    out_ref[...] = h_ref[...] + _dot(o_ref[...], w_ref[...])


def _oproj(h, o, w, *, tm=512):
    n, d = h.shape
    return pl.pallas_call(
        _oproj_kernel,
        grid=(n // tm,),
        in_specs=[
            pl.BlockSpec((tm, d), lambda i: (i, 0)),
            pl.BlockSpec((tm, o.shape[1]), lambda i: (i, 0)),
            pl.BlockSpec(w.shape, lambda i: (0, 0)),
        ],
        out_specs=pl.BlockSpec((tm, d), lambda i: (i, 0)),
        out_shape=jax.ShapeDtypeStruct((n, d), F32),
        compiler_params=_cparams("parallel"),
        name="oproj",
    )(h, o, w)


def _split_pair(x2):
    lane = lax.broadcasted_iota(jnp.int32, x2.shape, 1)
    xf = x2.astype(F32)
    return (jnp.where(lane < HEAD_DIM, xf, 0.0).astype(BF16),
            jnp.where(lane < HEAD_DIM, 0.0, xf).astype(BF16))


def _pair(ref, rows, p):
    return ref[0, rows, p * LANES:(p + 1) * LANES]


def _store_pairs(o_ref, outs):
    lane = lax.broadcasted_iota(jnp.int32, outs[0].shape, 1)
    for p in range(len(outs) // 2):
        o_ref[0, :, p * LANES:(p + 1) * LANES] = jnp.where(
            lane < HEAD_DIM, outs[2 * p], outs[2 * p + 1]).astype(o_ref.dtype)


def _sb_proj_kernel(h_ref, g_ref, w_ref, q_ref, k_ref, v_ref, *, scale):
    d = h_ref.shape[-1]
    xn = _rms(h_ref[...], g_ref[...]).astype(BF16)
    y = _dot(xn, w_ref[...])
    q_ref[...] = (y[:, :d] * scale).astype(BF16)
    k_ref[...] = y[:, d:2 * d].astype(BF16)
    v_ref[...] = y[:, 2 * d:].astype(BF16)


def _sb_proj(h, g, w, *, tm=512):
    n, d = h.shape
    out = jax.ShapeDtypeStruct((n, d), BF16)
    return pl.pallas_call(
        functools.partial(_sb_proj_kernel, scale=HEAD_DIM ** -0.5),
        grid=(n // tm,),
        in_specs=[
            pl.BlockSpec((tm, d), lambda i: (i, 0)),
            pl.BlockSpec((1, d), lambda i: (0, 0)),
            pl.BlockSpec(w.shape, lambda i: (0, 0)),
        ],
        out_specs=[pl.BlockSpec((tm, d), lambda i: (i, 0))] * 3,
        out_shape=[out, out, out],
        compiler_params=_cparams("parallel"),
        name="sb_proj",
    )(h, g.reshape(1, d), w)


def _sb_attn_kernel(q_ref, k_ref, v_ref, o_ref, *, t, g):
    i = pl.program_id(2)
    qs = []
    for p in range(g // 2):
        qs.extend(_split_pair(_pair(q_ref, slice(None), p)))
    row = lax.broadcasted_iota(jnp.int32, (t, t), 0)
    col = lax.broadcasted_iota(jnp.int32, (t, t), 1)
    tri = jnp.where(row > col, 1.0, 0.0).astype(BF16)
    tri2 = jnp.concatenate([tri, tri], axis=0)
    past = jnp.concatenate([col < row] * g, axis=0)

    def tile(j, carry, diag):
        c, os = carry
        rows = pl.ds(pl.multiple_of(j * t, t), t)
        z = jnp.concatenate([_dot_nt(qs[h], _pair(k_ref, rows, h // 2)) for h in range(g)], axis=0)
        sp = jnp.log(1.0 + jnp.exp(-jnp.abs(z)))
        log_beta = jnp.minimum(z, 0.0) - sp
        log_keep = log_beta - z
        if diag:
            log_keep = jnp.where(past, log_keep, 0.0)
        hi = log_keep.astype(BF16)
        lo = (log_keep - hi.astype(F32)).astype(BF16)
        tail = _dot(jnp.concatenate([hi, lo], axis=1), tri2) + c
        w = jnp.exp(log_beta + tail)
        if diag:
            w = jnp.where(past, w, 0.0)
        w = w.astype(BF16)
        os = tuple(os[h] + _dot(w[h * t:(h + 1) * t], _pair(v_ref, rows, h // 2)) for h in range(g))
        c = tail[:, :1] + log_keep[:, :1]
        return c, os

    init = (jnp.zeros((g * t, 1), F32), tuple(jnp.zeros((t, LANES), F32) for _ in range(g)))
    carry = tile(i, init, True)
    carry = lax.fori_loop(0, i, lambda s, cr: tile(i - 1 - s, cr, False), carry)
    _store_pairs(o_ref, list(carry[1]))


def _sb_attn(q, k, v, *, g=HEADS_PER_STEP):
    b, s, d = q.shape
    t, w = ATTN_TILE, g * HEAD_DIM
    qspec = pl.BlockSpec((1, t, w), lambda bi, hi, i: (bi, i, hi))
    kspec = pl.BlockSpec((1, s, w), lambda bi, hi, i: (bi, 0, hi))
    return pl.pallas_call(
        functools.partial(_sb_attn_kernel, t=t, g=g),
        grid=(b, d // w, s // t),
        in_specs=[qspec, kspec, kspec],
        out_specs=qspec,
        out_shape=jax.ShapeDtypeStruct((b, s, d), BF16),
        compiler_params=_cparams("parallel", "parallel", "arbitrary"),
        name="sb_attn",
    )(q, k, v)


def _sb_mixer(h, g, w_qkv, w_o, b, s):
    n, d = h.shape
    q, k, v = (x.reshape(b, s, d) for x in _sb_proj(h, g, w_qkv))
    return _oproj(h, _sb_attn(q, k, v).reshape(n, d), w_o)


def _softmax_attn_kernel(*refs, t, g, qk_lanes, has_sel):
    if has_sel:
        q_ref, k_ref, v_ref, sel_ref, o_ref = refs
    else:
        q_ref, k_ref, v_ref, o_ref = refs
    i = pl.program_id(2)
    row = lax.broadcasted_iota(jnp.int32, (t, t), 0)
    col = lax.broadcasted_iota(jnp.int32, (t, t), 1)
    causal = jnp.concatenate([col <= row] * g, axis=0)
    if qk_lanes == HEAD_DIM:
        qs = []
        for p in range(g // 2):
            qs.extend(_split_pair(_pair(q_ref, slice(None), p)))
        k_of = lambda rows, h: _pair(k_ref, rows, h // 2)
    else:
        qs = [_pair(q_ref, slice(None), h) for h in range(g)]
        k_of = lambda rows, h: _pair(k_ref, rows, h)
    if has_sel:
        sel = sel_ref[0].reshape(g * t, sel_ref.shape[-1])
        sel_lane = lax.broadcasted_iota(jnp.int32, sel.shape, 1)

    def tile(j, carry, diag):
        m, l, accs = carry
        rows = pl.ds(pl.multiple_of(j * t, t), t)
        sc = jnp.concatenate([_dot_nt(qs[h], k_of(rows, h)) for h in range(g)], axis=0)
        if diag:
            sc = jnp.where(causal, sc, NEG)
        elif has_sel:
            chosen = jnp.sum(jnp.where(sel_lane == j, sel, 0.0), axis=-1, keepdims=True)
            sc = jnp.where(chosen > 0.5, sc, NEG)
        m_new = jnp.maximum(m, jnp.max(sc, axis=-1, keepdims=True))
        alpha = jnp.exp2(m - m_new)
        p = jnp.exp2(sc - m_new)
        l = alpha * l + jnp.sum(p, axis=-1, keepdims=True)
        p = p.astype(BF16)
        accs = tuple(alpha[h * t:(h + 1) * t] * accs[h] + _dot(p[h * t:(h + 1) * t], _pair(v_ref, rows, h // 2))
                     for h in range(g))
        return m_new, l, accs

    init = (jnp.full((g * t, 1), NEG, F32), jnp.zeros((g * t, 1), F32),
            tuple(jnp.zeros((t, LANES), F32) for _ in range(g)))
    carry = tile(i, init, True)
    _, l, accs = lax.fori_loop(0, i, lambda j, cr: tile(j, cr, False), carry)
    _store_pairs(o_ref, [accs[h] / l[h * t:(h + 1) * t] for h in range(g)])


def _softmax_attn(q, k, v, sel, *, qk_lanes, name, g=HEADS_PER_STEP):
    b, s, dv = v.shape
    t = ATTN_TILE
    wq, wv = g * qk_lanes, g * HEAD_DIM
    in_specs = [pl.BlockSpec((1, t, wq), lambda bi, hi, i: (bi, i, hi)),
                pl.BlockSpec((1, s, wq), lambda bi, hi, i: (bi, 0, hi)),
                pl.BlockSpec((1, s, wv), lambda bi, hi, i: (bi, 0, hi))]
    args = [q, k, v]
    if sel is not None:
        in_specs.append(pl.BlockSpec((1, g, t, sel.shape[-1]), lambda bi, hi, i: (bi, hi, i, 0)))
        args.append(sel)
    return pl.pallas_call(
        functools.partial(_softmax_attn_kernel, t=t, g=g, qk_lanes=qk_lanes, has_sel=sel is not None),
        grid=(b, dv // wv, s // t),
        in_specs=in_specs,
        out_specs=pl.BlockSpec((1, t, wv), lambda bi, hi, i: (bi, i, hi)),
        out_shape=jax.ShapeDtypeStruct((b, s, dv), BF16),
        compiler_params=_cparams("parallel", "parallel", "arbitrary"),
        name=name,
    )(*args)


def _rot_tables(s, r, lead, width):
    inv = ROPE_THETA ** (-jnp.arange(0, r, 2, dtype=F32) / r)
    ang = jnp.arange(s).astype(F32)[:, None] * inv[None, :]
    cos, sin = jnp.cos(ang), jnp.sin(ang)
    one = lambda w: jnp.ones((s, w), F32)
    zero = lambda w: jnp.zeros((s, w), F32)
    rest = width - lead - r
    return (jnp.concatenate([one(lead), cos, cos, one(rest)], axis=-1),
            jnp.concatenate([zero(lead), -sin, sin, zero(rest)], axis=-1))


def _swap_perm(r, width, n):
    d = jnp.arange(width)
    p = jnp.where(d < r // 2, d + r // 2, jnp.where(d < r, d - r // 2, d))
    return (jnp.arange(n)[:, None] * width + p[None, :]).reshape(-1)


def _moba_proj_kernel(h_ref, g_ref, w_ref, cos_ref, sin_ref, qs_ref, qf_ref, k_ref, v_ref, km_ref, *, scale):
    d = h_ref.shape[-1]
    xn = _rms(h_ref[0], g_ref[...]).astype(BF16)
    y = _dot(xn, w_ref[...])
    cos = jnp.tile(cos_ref[...], (1, d // LANES))
    sin = jnp.tile(sin_ref[...], (1, d // LANES))
    q = y[:, :d] * cos + y[:, d:2 * d] * sin
    k = y[:, 2 * d:3 * d] * cos + y[:, 3 * d:4 * d] * sin
    qs_ref[0] = (q * scale).astype(BF16)
    qf_ref[0] = q
    k_ref[0] = k.astype(BF16)
    v_ref[0] = y[:, 4 * d:].astype(BF16)
    km_ref[0, 0] = jnp.mean(k, axis=0, keepdims=True)


def _moba_proj(h3, g, w_ext, cos, sin):
    b, s, d = h3.shape
    tm = MOBA_BLOCK
    nb = s // tm
    tok = lambda dt: jax.ShapeDtypeStruct((b, s, d), dt)
    tok_spec = pl.BlockSpec((1, tm, d), lambda bi, i: (bi, i, 0))
    tab_spec = pl.BlockSpec((tm, cos.shape[1]), lambda bi, i: (i, 0))
    return pl.pallas_call(
        functools.partial(_moba_proj_kernel, scale=HEAD_DIM ** -0.5 * LOG2E),
        grid=(b, nb),
        in_specs=[
            tok_spec,
            pl.BlockSpec((1, d), lambda bi, i: (0, 0)),
            pl.BlockSpec(w_ext.shape, lambda bi, i: (0, 0)),
            tab_spec, tab_spec,
        ],
        out_specs=[tok_spec, tok_spec, tok_spec, tok_spec,
                   pl.BlockSpec((1, 1, 1, d), lambda bi, i: (bi, i, 0, 0))],
        out_shape=[tok(BF16), tok(F32), tok(BF16), tok(BF16),
                   jax.ShapeDtypeStruct((b, nb, 1, d), F32)],
        compiler_params=_cparams("parallel", "parallel"),
        name="moba_proj",
    )(h3, g.reshape(1, d), w_ext, cos, sin)


def _moba_select_kernel(q_ref, km_ref, sel_ref, *, ts, nb):
    t0 = pl.program_id(1) * ts
    q = q_ref[0]
    km = km_ref[0]
    r = lax.broadcasted_iota(jnp.int32, km.shape, 0)
    c = lax.broadcasted_iota(jnp.int32, km.shape, 1)
    km = jnp.where((r % N_HEADS) == (c // HEAD_DIM), km, 0.0)
    qh = q.astype(BF16)
    ql = (q - qh.astype(F32)).astype(BF16)
    kh = km.astype(BF16)
    kl = (km - kh.astype(F32)).astype(BF16)
    gate = _dot_nt(kh, qh) + _dot_nt(kh, ql) + _dot_nt(kl, qh)
    qblk = (t0 + lax.broadcasted_iota(jnp.int32, (N_HEADS, ts), 1)) // MOBA_BLOCK
    gs = [jnp.where(n < qblk, gate[n * N_HEADS:(n + 1) * N_HEADS, :], NEG) for n in range(nb)]
    for n in range(nb):
        rank = jnp.zeros((N_HEADS, ts), F32)
        for m in range(nb):
            if m == n:
                continue
            beats = (gs[m] >= gs[n]) if m < n else (gs[m] > gs[n])
            rank = rank + jnp.where(beats, 1.0, 0.0)
        keep = jnp.where(rank < MOBA_TOPK - 0.5, 1.0, 0.0)
        sel_ref[0, n] = jnp.where(n < qblk, keep, 0.0)


def _moba_select(qf, km_rep, *, ts=512):
    b, s, d = qf.shape
    nb = s // MOBA_BLOCK
    return pl.pallas_call(
        functools.partial(_moba_select_kernel, ts=ts, nb=nb),
        grid=(b, s // ts),
        in_specs=[
            pl.BlockSpec((1, ts, d), lambda bi, i: (bi, i, 0)),
            pl.BlockSpec((1, nb * N_HEADS, d), lambda bi, i: (bi, 0, 0)),
        ],
        out_specs=pl.BlockSpec((1, nb, N_HEADS, ts), lambda bi, i: (bi, 0, 0, i)),
        out_shape=jax.ShapeDtypeStruct((b, nb, N_HEADS, s), F32),
        compiler_params=_cparams("parallel", "parallel"),
        name="moba_select",
    )(qf, km_rep)


def _moba_mixer(h, g, w_qkv, w_o, b, s):
    n, d = h.shape
    assert s % MOBA_BLOCK == 0 and MOBA_BLOCK == ATTN_TILE
    nb = s // MOBA_BLOCK
    perm = _swap_perm(ROT_DIM, HEAD_DIM, N_HEADS)
    wq, wk, wv = w_qkv[:, :d], w_qkv[:, d:2 * d], w_qkv[:, 2 * d:]
    w_ext = jnp.concatenate([wq, wq[:, perm], wk, wk[:, perm], wv], axis=1).astype(BF16)
    cos, sin = (jnp.tile(x, (1, LANES // HEAD_DIM)) for x in _rot_tables(s, ROT_DIM, 0, HEAD_DIM))
    qs, qf, k, v, km = _moba_proj(h.reshape(b, s, d), g, w_ext, cos, sin)
    km_rep = jnp.repeat(km.reshape(b, nb, d), N_HEADS, axis=1)
    sel = _moba_select(qf, km_rep).transpose(0, 2, 3, 1)
    o = _softmax_attn(qs, k, v, sel, qk_lanes=HEAD_DIM, name="moba_attn")
    return _oproj(h, o.reshape(n, d), w_o)


def _mla_proj_kernel(h_ref, g_ref, win_ref, gq_ref, gkv_ref, wuq_ref, wuqs_ref, wukv_ref, e_ref,
                     cq_ref, sq_ref, ck_ref, sk_ref, q_ref, k_ref, v_ref, *, scale):
    hk = N_HEADS * LANES
    xn = _rms(h_ref[0], g_ref[...]).astype(BF16)
    c = _dot(xn, win_ref[...])
    a0, a1, a2 = MLA_Q_RANK, MLA_Q_RANK + MLA_KV_RANK, MLA_Q_RANK + MLA_KV_RANK + MLA_ROPE
    cq = _rms(c[:, :a0], gq_ref[...]).astype(BF16)
    ckv = _rms(c[:, a0:a1], gkv_ref[...]).astype(BF16)
    kr = c[:, a1:a2] * ck_ref[...] + c[:, a2:a2 + MLA_ROPE] * sk_ref[...]
    cos = jnp.tile(cq_ref[...], (1, N_HEADS))
    sin = jnp.tile(sq_ref[...], (1, N_HEADS))
    q = _dot(cq, wuq_ref[...]) * cos + _dot(cq, wuqs_ref[...]) * sin
    kv = _dot(ckv, wukv_ref[...])
    k = kv[:, :hk] + _dot(kr.astype(BF16), e_ref[...])
    q_ref[0] = (q * scale).astype(BF16)
    k_ref[0] = k.astype(BF16)
    v_ref[0] = kv[:, hk:].astype(BF16)


def _mla_proj(h3, g, w_in_ext, gq, gkv, w_uq_ext, w_uq_sw, w_ukv_ext, e, cq, sq, ck, sk, *, tm=256):
    b, s, d = h3.shape
    hk, hv = N_HEADS * LANES, N_HEADS * MLA_V
    full = lambda a: pl.BlockSpec(a.shape, lambda bi, i: (0,) * a.ndim)
    tok = lambda w: pl.BlockSpec((1, tm, w), lambda bi, i: (bi, i, 0))
    tab = lambda a: pl.BlockSpec((tm, a.shape[1]), lambda bi, i: (i, 0))
    gq2, gkv2, g2 = gq.reshape(1, -1), gkv.reshape(1, -1), g.reshape(1, d)
    return pl.pallas_call(
        functools.partial(_mla_proj_kernel, scale=(MLA_NOPE + MLA_ROPE) ** -0.5 * LOG2E),
        grid=(b, s // tm),
        in_specs=[tok(d), full(g2), full(w_in_ext), full(gq2), full(gkv2), full(w_uq_ext), full(w_uq_sw),
                  full(w_ukv_ext), full(e), tab(cq), tab(sq), tab(ck), tab(sk)],
        out_specs=[tok(hk), tok(hk), tok(hv)],
        out_shape=[jax.ShapeDtypeStruct((b, s, w), BF16) for w in (hk, hk, hv)],
        compiler_params=_cparams("parallel", "parallel"),
        name="mla_proj",
    )(h3, g2, w_in_ext, gq2, gkv2, w_uq_ext, w_uq_sw, w_ukv_ext, e, cq, sq, ck, sk)


def _mla_mixer(h, g, w_in, q_norm, w_uq, kv_norm, w_ukv, w_o, b, s):
    n, d = h.shape
    a1 = MLA_Q_RANK + MLA_KV_RANK
    pad = LANES - MLA_NOPE - MLA_ROPE
    swap = _swap_perm(MLA_ROPE, MLA_ROPE, 1)
    w_in_ext = jnp.concatenate(
        [w_in, w_in[:, a1:][:, swap], jnp.zeros((d, 768 - a1 - 2 * MLA_ROPE), w_in.dtype)], axis=1).astype(BF16)
    wq = w_uq.reshape(MLA_Q_RANK, N_HEADS, MLA_NOPE + MLA_ROPE)
    zq = lambda w: jnp.zeros((MLA_Q_RANK, N_HEADS, w), w_uq.dtype)
    w_uq_ext = jnp.concatenate([wq, zq(pad)], axis=2).reshape(MLA_Q_RANK, -1).astype(BF16)
    w_uq_sw = jnp.concatenate([zq(MLA_NOPE), wq[:, :, MLA_NOPE:][:, :, swap], zq(pad)],
                              axis=2).reshape(MLA_Q_RANK, -1).astype(BF16)
    wkv = w_ukv.reshape(MLA_KV_RANK, N_HEADS, MLA_NOPE + MLA_V)
    w_k = jnp.concatenate([wkv[:, :, :MLA_NOPE], jnp.zeros((MLA_KV_RANK, N_HEADS, LANES - MLA_NOPE), w_ukv.dtype)],
                          axis=2).reshape(MLA_KV_RANK, -1)
    w_ukv_ext = jnp.concatenate([w_k, wkv[:, :, MLA_NOPE:].reshape(MLA_KV_RANK, -1)], axis=1).astype(BF16)
    eye = jnp.broadcast_to(jnp.eye(MLA_ROPE, dtype=BF16)[:, None, :], (MLA_ROPE, N_HEADS, MLA_ROPE))
    ze = lambda w: jnp.zeros((MLA_ROPE, N_HEADS, w), BF16)
    e = jnp.concatenate([ze(MLA_NOPE), eye, ze(pad)], axis=2).reshape(MLA_ROPE, -1)
    ck, sk = _rot_tables(s, MLA_ROPE, 0, MLA_ROPE)
    cq, sq = _rot_tables(s, MLA_ROPE, MLA_NOPE, LANES)
    q, k, v = _mla_proj(h.reshape(b, s, d), g, w_in_ext, q_norm, kv_norm, w_uq_ext, w_uq_sw, w_ukv_ext, e,
                        cq, sq, ck, sk)
    o = _softmax_attn(q, k, v, None, qk_lanes=LANES, name="mla_attn")
    return _oproj(h, o.reshape(n, d), w_o)


def kernel(x, norm_ffn1, ffn1_w_gate_up, ffn1_w_down, norm_mix, norm_ffn2, ffn2_w_gate_up, ffn2_w_down,
           sb_w_qkv, sb_w_o, moba_w_qkv, moba_w_o, mla_w_in, mla_q_norm, mla_w_uq, mla_kv_norm,
           mla_w_ukv, mla_w_o, final_norm):
    b, s, d = x.shape
    h = x.reshape(b * s, d)
    bf = lambda w: w.astype(BF16)
    for i in range(DEPTH):
        h = _ffn(h, norm_ffn1[i], bf(ffn1_w_gate_up[i]), bf(ffn1_w_down[i]), final_norm, final=False)
        kind, j = i % 3, i // 3
        if kind == 0:
            h = _sb_mixer(h, norm_mix[i], bf(sb_w_qkv[j]), bf(sb_w_o[j]), b, s)
        elif kind == 1:
            h = _moba_mixer(h, norm_mix[i], moba_w_qkv[j], bf(moba_w_o[j]), b, s)
        else:
            h = _mla_mixer(h, norm_mix[i], mla_w_in[j], mla_q_norm[j], mla_w_uq[j], mla_kv_norm[j],
                           mla_w_ukv[j], bf(mla_w_o[j]), b, s)
        h = _ffn(h, norm_ffn2[i], bf(ffn2_w_gate_up[i]), bf(ffn2_w_down[i]), final_norm,
                 final=(i == DEPTH - 1))
    return h.reshape(b, s, d)
```

```python
import functools

import jax
import jax.numpy as jnp
from jax import lax
from jax.experimental import pallas as pl
from jax.experimental.pallas import tpu as pltpu

F32 = jnp.float32
BF16 = jnp.bfloat16

DEPTH = 4
N_HEADS = 16
HEAD_DIM = 64
ROPE_THETA = 500000.0
ROT_DIM = HEAD_DIM // 4
EPS = 1e-6
NEG = -1e30
LOG2E = 1.4426950408889634

MOBA_BLOCK = 256
MOBA_TOPK = 3

MLA_Q_RANK = 384
MLA_KV_RANK = 256
MLA_NOPE = 64
MLA_ROPE = 32
MLA_V = 64

LANES = 128
ATTN_TILE = 256
HEADS_PER_STEP = 8
VMEM_LIMIT = 48 * 1024 * 1024

_NT = (((1,), (1,)), ((), ()))


def _cparams(*sem):
    return pltpu.CompilerParams(dimension_semantics=sem, vmem_limit_bytes=VMEM_LIMIT)


def _rms(x, g):
    ms = jnp.mean(x * x, axis=-1, keepdims=True)
    return x * lax.rsqrt(ms + EPS) * g


def _dot(a, b):
    return jnp.dot(a, b, preferred_element_type=F32)


def _dot_nt(a, b):
    return lax.dot_general(a, b, _NT, preferred_element_type=F32)


def _ffn_kernel(h_ref, g_ref, wgu_ref, wd_ref, gf_ref, o_ref, *, final):
    dff = wd_ref.shape[0]
    h = h_ref[...]
    gu = _dot(_rms(h, g_ref[...]).astype(BF16), wgu_ref[...])
    gate, up = gu[:, :dff], gu[:, dff:]
    act = (gate * jax.nn.sigmoid(gate) * up).astype(BF16)
    y = h + 0.5 * _dot(act, wd_ref[...])
    if final:
        y = _rms(y, gf_ref[...])
    o_ref[...] = y


def _ffn(h, g, w_gate_up, w_down, g_final, *, final, tm=512):
    n, d = h.shape
    assert n % tm == 0
    resident = lambda a: pl.BlockSpec(a.shape, lambda i: (0, 0), pipeline_mode=pl.Buffered(1))
    return pl.pallas_call(
        functools.partial(_ffn_kernel, final=final),
        grid=(n // tm,),
        in_specs=[
            pl.BlockSpec((tm, d), lambda i: (i, 0)),
            pl.BlockSpec((1, d), lambda i: (0, 0)),
            resident(w_gate_up),
            resident(w_down),
            pl.BlockSpec((1, d), lambda i: (0, 0)),
        ],
        out_specs=pl.BlockSpec((tm, d), lambda i: (i, 0)),
        out_shape=jax.ShapeDtypeStruct((n, d), F32),
        compiler_params=_cparams("parallel"),
        name="ffn_final" if final else "ffn",
    )(h, g.reshape(1, d), w_gate_up, w_down, g_final.reshape(1, d))


def _oproj_kernel(h_ref, o_ref, w_ref, out_ref):
    out_ref[...] = h_ref[...] + _dot(o_ref[...], w_ref[...])


def _oproj(h, o, w, *, tm=512):
    n, d = h.shape
    return pl.pallas_call(
        _oproj_kernel,
        grid=(n // tm,),
        in_specs=[
            pl.BlockSpec((tm, d), lambda i: (i, 0)),
            pl.BlockSpec((tm, o.shape[1]), lambda i: (i, 0)),
            pl.BlockSpec(w.shape, lambda i: (0, 0)),
        ],
        out_specs=pl.BlockSpec((tm, d), lambda i: (i, 0)),
        out_shape=jax.ShapeDtypeStruct((n, d), F32),
        compiler_params=_cparams("parallel"),
        name="oproj",
    )(h, o, w)


def _full(a):
    return pl.BlockSpec(a.shape, lambda bi, i: (0,) * a.ndim)


def _tok_spec(width):
    return pl.BlockSpec((1, ATTN_TILE, width), lambda bi, i: (bi, i, 0))


def _tile_major_spec(rows):
    return pl.BlockSpec((1, 1, rows, ATTN_TILE), lambda bi, i: (bi, i, 0, 0))


def _tile_major_shape(b, s, rows, dtype):
    return jax.ShapeDtypeStruct((b, s // ATTN_TILE, rows, ATTN_TILE), dtype)


def _split_pair_t(x2):
    r = lax.broadcasted_iota(jnp.int32, x2.shape, 0)
    xf = x2.astype(F32)
    return (jnp.where(r < HEAD_DIM, xf, 0.0).astype(BF16),
            jnp.where(r < HEAD_DIM, 0.0, xf).astype(BF16))


def _rows128(ref, lead, p):
    return ref[lead + (slice(p * LANES, (p + 1) * LANES), slice(None))]


def _keys(k_ref, rows, p):
    return k_ref[0, rows, p * LANES:(p + 1) * LANES]


def _store_pairs_t(o_ref, outs):
    r = lax.broadcasted_iota(jnp.int32, outs[0].shape, 0)
    for p in range(len(outs) // 2):
        o2 = jnp.where(r < HEAD_DIM, outs[2 * p], outs[2 * p + 1])
        o_ref[0, :, p * LANES:(p + 1) * LANES] = o2.T.astype(o_ref.dtype)


def _sb_proj_kernel(h_ref, g_ref, wqt_ref, wk_ref, wvt_ref, qt_ref, k_ref, vt_ref, *, scale):
    xn = _rms(h_ref[0], g_ref[...]).astype(BF16)
    qt_ref[0, 0] = (_dot_nt(wqt_ref[...], xn) * scale).astype(BF16)
    k_ref[0] = _dot(xn, wk_ref[...]).astype(BF16)
    vt_ref[0, 0] = _dot_nt(wvt_ref[...], xn).astype(BF16)


def _sb_proj(h3, g, wqt, wk, wvt):
    b, s, d = h3.shape
    g2 = g.reshape(1, d)
    return pl.pallas_call(
        functools.partial(_sb_proj_kernel, scale=HEAD_DIM ** -0.5),
        grid=(b, s // ATTN_TILE),
        in_specs=[_tok_spec(d), _full(g2), _full(wqt), _full(wk), _full(wvt)],
        out_specs=[_tile_major_spec(d), _tok_spec(d), _tile_major_spec(d)],
        out_shape=[_tile_major_shape(b, s, d, BF16), jax.ShapeDtypeStruct((b, s, d), BF16),
                   _tile_major_shape(b, s, d, BF16)],
        compiler_params=_cparams("parallel", "parallel"),
        name="sb_proj",
    )(h3, g2, wqt, wk, wvt)


def _sb_attn_kernel(qt_ref, k_ref, vt_ref, o_ref, *, t, g):
    i = pl.program_id(2)
    key = lax.broadcasted_iota(jnp.int32, (t, t), 0)
    qry = lax.broadcasted_iota(jnp.int32, (t, t), 1)
    tri = jnp.where(qry > key, 1.0, 0.0).astype(BF16)
    tri2 = jnp.concatenate([tri, tri], axis=1)
    past = key < qry
    qts = []
    for p in range(g // 2):
        qts.extend(_split_pair_t(_rows128(qt_ref, (0, 0), p)))

    def tile(j, carry, diag):
        cs, os = carry
        rows = pl.ds(pl.multiple_of(j * t, t), t)
        hs = range(g)
        zs = [_dot(_keys(k_ref, rows, h // 2), qts[h]) for h in hs]
        sps = [jnp.log(1.0 + jnp.exp2(jnp.abs(z) * -LOG2E)) for z in zs]
        lbs = [jnp.minimum(zs[h], 0.0) - sps[h] for h in hs]
        lks = [lbs[h] - zs[h] for h in hs]
        if diag:
            lks = [jnp.where(past, lk, 0.0) for lk in lks]
        his = [lk.astype(BF16) for lk in lks]
        los = [(lks[h] - his[h].astype(F32)).astype(BF16) for h in hs]
        tails = [_dot(tri2, jnp.concatenate([his[h], los[h]], axis=0)) + cs[h] for h in hs]
        ws = [jnp.exp(lbs[h] + tails[h]) for h in hs]
        if diag:
            ws = [jnp.where(past, w, 0.0) for w in ws]
        os = tuple(os[h] + _dot(_rows128(vt_ref, (0, j), h // 2), ws[h].astype(BF16)) for h in hs)
        cs = tuple(tails[h][:1, :] + lks[h][:1, :] for h in hs)
        return cs, os

    init = (tuple(jnp.zeros((1, t), F32) for _ in range(g)),
            tuple(jnp.zeros((LANES, t), F32) for _ in range(g)))
    carry = tile(i, init, True)
    carry = lax.fori_loop(0, i, lambda s, cr: tile(i - 1 - s, cr, False), carry)
    _store_pairs_t(o_ref, list(carry[1]))


def _attn_specs(s, t, g, qk_rows):
    wq, wv = g * qk_rows, g * HEAD_DIM
    nt = s // t
    return ([pl.BlockSpec((1, 1, wq, t), lambda bi, hi, i: (bi, i, hi, 0)),
             pl.BlockSpec((1, s, wq), lambda bi, hi, i: (bi, 0, hi)),
             pl.BlockSpec((1, nt, wv, t), lambda bi, hi, i: (bi, 0, hi, 0))],
            pl.BlockSpec((1, t, wv), lambda bi, hi, i: (bi, i, hi)))


def _sb_attn(qt, k, vt, *, g=HEADS_PER_STEP):
    b, s, d = k.shape
    t = ATTN_TILE
    in_specs, out_spec = _attn_specs(s, t, g, HEAD_DIM)
    return pl.pallas_call(
        functools.partial(_sb_attn_kernel, t=t, g=g),
        grid=(b, N_HEADS // g, s // t),
        in_specs=in_specs,
        out_specs=out_spec,
        out_shape=jax.ShapeDtypeStruct((b, s, d), BF16),
        compiler_params=_cparams("parallel", "parallel", "arbitrary"),
        name="sb_attn",
    )(qt, k, vt)


def _sb_mixer(h, g, w_qkv, w_o, b, s):
    n, d = h.shape
    wqt, wk, wvt = w_qkv[:, :d].T.astype(BF16), w_qkv[:, d:2 * d].astype(BF16), w_qkv[:, 2 * d:].T.astype(BF16)
    qt, k, vt = _sb_proj(h.reshape(b, s, d), g, wqt, wk, wvt)
    return _oproj(h, _sb_attn(qt, k, vt).reshape(n, d), w_o)


def _softmax_attn_kernel(*refs, t, g, qk_rows, has_sel):
    if has_sel:
        qt_ref, k_ref, vt_ref, sel_ref, o_ref = refs
    else:
        qt_ref, k_ref, vt_ref, o_ref = refs
    i = pl.program_id(2)
    key = lax.broadcasted_iota(jnp.int32, (t, t), 0)
    qry = lax.broadcasted_iota(jnp.int32, (t, t), 1)
    causal = key <= qry
    if qk_rows == HEAD_DIM:
        qts = []
        for p in range(g // 2):
            qts.extend(_split_pair_t(_rows128(qt_ref, (0, 0), p)))
        k_of = lambda rows, h: _keys(k_ref, rows, h // 2)
    else:
        qts = [_rows128(qt_ref, (0, 0), h) for h in range(g)]
        k_of = lambda rows, h: _keys(k_ref, rows, h)

    def tile(j, carry, diag):
        ms, ls, accs = carry
        rows = pl.ds(pl.multiple_of(j * t, t), t)
        hs = range(g)
        scs = [_dot(k_of(rows, h), qts[h]) for h in hs]
        if diag:
            scs = [jnp.where(causal, sc, NEG) for sc in scs]
        elif has_sel:
            sel = sel_ref[0, j]
            scs = [jnp.where(sel[h:h + 1, :] > 0.5, scs[h], NEG) for h in hs]
        mns = [jnp.maximum(ms[h], jnp.max(scs[h], axis=0, keepdims=True)) for h in hs]
        alphas = [jnp.exp2(ms[h] - mns[h]) for h in hs]
        ps = [jnp.exp2(scs[h] - mns[h]) for h in hs]
        ls = tuple(alphas[h] * ls[h] + jnp.sum(ps[h], axis=0, keepdims=True) for h in hs)
        accs = tuple(alphas[h] * accs[h] + _dot(_rows128(vt_ref, (0, j), h // 2), ps[h].astype(BF16))
                     for h in hs)
        return tuple(mns), ls, accs

    init = (tuple(jnp.full((1, t), NEG, F32) for _ in range(g)),
            tuple(jnp.zeros((1, t), F32) for _ in range(g)),
            tuple(jnp.zeros((LANES, t), F32) for _ in range(g)))
    carry = tile(i, init, True)
    _, ls, accs = lax.fori_loop(0, i, lambda j, cr: tile(j, cr, False), carry)
    _store_pairs_t(o_ref, [accs[h] / ls[h] for h in range(g)])


def _softmax_attn(qt, k, vt, sel, *, qk_rows, name, g=HEADS_PER_STEP):
    b, s, _ = k.shape
    t = ATTN_TILE
    in_specs, out_spec = _attn_specs(s, t, g, qk_rows)
    args = [qt, k, vt]
    if sel is not None:
        in_specs.append(pl.BlockSpec((1, sel.shape[1], g, t), lambda bi, hi, i: (bi, 0, hi, i)))
        args.append(sel)
    return pl.pallas_call(
        functools.partial(_softmax_attn_kernel, t=t, g=g, qk_rows=qk_rows, has_sel=sel is not None),
        grid=(b, N_HEADS // g, s // t),
        in_specs=in_specs,
        out_specs=out_spec,
        out_shape=jax.ShapeDtypeStruct((b, s, N_HEADS * HEAD_DIM), BF16),
        compiler_params=_cparams("parallel", "parallel", "arbitrary"),
        name=name,
    )(*args)


def _rot_tables(s, r, lead, width):
    inv = ROPE_THETA ** (-jnp.arange(0, r, 2, dtype=F32) / r)
    ang = jnp.arange(s).astype(F32)[:, None] * inv[None, :]
    cos, sin = jnp.cos(ang), jnp.sin(ang)
    one = lambda w: jnp.ones((s, w), F32)
    zero = lambda w: jnp.zeros((s, w), F32)
    rest = width - lead - r
    return (jnp.concatenate([one(lead), cos, cos, one(rest)], axis=-1),
            jnp.concatenate([zero(lead), -sin, sin, zero(rest)], axis=-1))


def _swap_perm(r, width, n):
    d = jnp.arange(width)
    p = jnp.where(d < r // 2, d + r // 2, jnp.where(d < r, d - r // 2, d))
    return (jnp.arange(n)[:, None] * width + p[None, :]).reshape(-1)


def _moba_proj_kernel(h_ref, g_ref, wqt_ref, wqst_ref, wk_ref, wks_ref, wvt_ref,
                      cost_ref, sint_ref, cos_ref, sin_ref,
                      qt_ref, qft_ref, k_ref, vt_ref, km_ref, *, scale):
    reps = h_ref.shape[-1] // LANES
    xn = _rms(h_ref[0], g_ref[...]).astype(BF16)
    cost, sint = jnp.tile(cost_ref[...], (reps, 1)), jnp.tile(sint_ref[...], (reps, 1))
    cos, sin = jnp.tile(cos_ref[...], (1, reps)), jnp.tile(sin_ref[...], (1, reps))
    qt = _dot_nt(wqt_ref[...], xn) * cost + _dot_nt(wqst_ref[...], xn) * sint
    k = _dot(xn, wk_ref[...]) * cos + _dot(xn, wks_ref[...]) * sin
    qt_ref[0, 0] = (qt * scale).astype(BF16)
    qft_ref[0, 0] = qt
    k_ref[0] = k.astype(BF16)
    vt_ref[0, 0] = _dot_nt(wvt_ref[...], xn).astype(BF16)
    km_ref[0, 0] = jnp.mean(k, axis=0, keepdims=True)


def _moba_proj(h3, g, wqt, wqst, wk, wks, wvt, cost, sint, cos, sin):
    b, s, d = h3.shape
    t = ATTN_TILE
    g2 = g.reshape(1, d)
    tab_t = pl.BlockSpec((LANES, t), lambda bi, i: (0, i))
    tab = pl.BlockSpec((t, LANES), lambda bi, i: (i, 0))
    return pl.pallas_call(
        functools.partial(_moba_proj_kernel, scale=HEAD_DIM ** -0.5 * LOG2E),
        grid=(b, s // t),
        in_specs=[_tok_spec(d), _full(g2), _full(wqt), _full(wqst), _full(wk), _full(wks), _full(wvt),
                  tab_t, tab_t, tab, tab],
        out_specs=[_tile_major_spec(d), _tile_major_spec(d), _tok_spec(d), _tile_major_spec(d),
                   pl.BlockSpec((1, 1, 1, d), lambda bi, i: (bi, i, 0, 0))],
        out_shape=[_tile_major_shape(b, s, d, BF16), _tile_major_shape(b, s, d, F32),
                   jax.ShapeDtypeStruct((b, s, d), BF16), _tile_major_shape(b, s, d, BF16),
                   jax.ShapeDtypeStruct((b, s // t, 1, d), F32)],
        compiler_params=_cparams("parallel", "parallel"),
        name="moba_proj",
    )(h3, g2, wqt, wqst, wk, wks, wvt, cost, sint, cos, sin)


def _moba_select_kernel(qt_ref, km_ref, sel_ref, *, nb):
    t = qt_ref.shape[-1]
    t0 = pl.program_id(1) * t
    qt = qt_ref[0, 0]
    km = km_ref[0]
    r = lax.broadcasted_iota(jnp.int32, km.shape, 0)
    c = lax.broadcasted_iota(jnp.int32, km.shape, 1)
    km = jnp.where((r % N_HEADS) == (c // HEAD_DIM), km, 0.0)
    qh = qt.astype(BF16)
    ql = (qt - qh.astype(F32)).astype(BF16)
    kh = km.astype(BF16)
    kl = (km - kh.astype(F32)).astype(BF16)
    gate = _dot(kh, qh) + _dot(kh, ql) + _dot(kl, qh)
    qblk = (t0 + lax.broadcasted_iota(jnp.int32, (N_HEADS, t), 1)) // MOBA_BLOCK
    gs = [jnp.where(n < qblk, gate[n * N_HEADS:(n + 1) * N_HEADS, :], NEG) for n in range(nb)]
    for n in range(nb):
        rank = jnp.zeros((N_HEADS, t), F32)
        for m in range(nb):
            if m == n:
                continue
            beats = (gs[m] >= gs[n]) if m < n else (gs[m] > gs[n])
            rank = rank + jnp.where(beats, 1.0, 0.0)
        keep = jnp.where(rank < MOBA_TOPK - 0.5, 1.0, 0.0)
        sel_ref[0, n] = jnp.where(n < qblk, keep, 0.0)


def _moba_select(qft, km_rep):
    b, nt, d, t = qft.shape
    s = nt * t
    nb = s // MOBA_BLOCK
    return pl.pallas_call(
        functools.partial(_moba_select_kernel, nb=nb),
        grid=(b, nt),
        in_specs=[_tile_major_spec(d), pl.BlockSpec((1, nb * N_HEADS, d), lambda bi, i: (bi, 0, 0))],
        out_specs=pl.BlockSpec((1, nb, N_HEADS, t), lambda bi, i: (bi, 0, 0, i)),
        out_shape=jax.ShapeDtypeStruct((b, nb, N_HEADS, s), F32),
        compiler_params=_cparams("parallel", "parallel"),
        name="moba_select",
    )(qft, km_rep)


def _moba_mixer(h, g, w_qkv, w_o, b, s):
    n, d = h.shape
    assert s % MOBA_BLOCK == 0 and MOBA_BLOCK == ATTN_TILE
    nb = s // MOBA_BLOCK
    perm = _swap_perm(ROT_DIM, HEAD_DIM, N_HEADS)
    wq, wk, wv = w_qkv[:, :d], w_qkv[:, d:2 * d], w_qkv[:, 2 * d:]
    bf = lambda w: w.astype(BF16)
    cos, sin = (jnp.tile(x, (1, LANES // HEAD_DIM)) for x in _rot_tables(s, ROT_DIM, 0, HEAD_DIM))
    qt, qft, k, vt, km = _moba_proj(h.reshape(b, s, d), g, bf(wq.T), bf(wq[:, perm].T), bf(wk), bf(wk[:, perm]),
                                    bf(wv.T), cos.T, sin.T, cos, sin)
    km_rep = jnp.repeat(km.reshape(b, nb, d), N_HEADS, axis=1)
    sel = _moba_select(qft, km_rep)
    o = _softmax_attn(qt, k, vt, sel, qk_rows=HEAD_DIM, name="moba_attn")
    return _oproj(h, o.reshape(n, d), w_o)


def _mla_proj_kernel(h_ref, g_ref, win_ref, gq_ref, gkv_ref, wuqt_ref, wuqst_ref, wk_ref, wvt_ref, e_ref,
                     cqt_ref, sqt_ref, ck_ref, sk_ref, qt_ref, k_ref, vt_ref, *, scale):
    xn = _rms(h_ref[0], g_ref[...]).astype(BF16)
    c = _dot(xn, win_ref[...])
    a0, a1, a2 = MLA_Q_RANK, MLA_Q_RANK + MLA_KV_RANK, MLA_Q_RANK + MLA_KV_RANK + MLA_ROPE
    cq = _rms(c[:, :a0], gq_ref[...]).astype(BF16)
    ckv = _rms(c[:, a0:a1], gkv_ref[...]).astype(BF16)
    kr = c[:, a1:a2] * ck_ref[...] + c[:, a2:a2 + MLA_ROPE] * sk_ref[...]
    cost, sint = jnp.tile(cqt_ref[...], (N_HEADS, 1)), jnp.tile(sqt_ref[...], (N_HEADS, 1))
    qt = _dot_nt(wuqt_ref[...], cq) * cost + _dot_nt(wuqst_ref[...], cq) * sint
    k = _dot(ckv, wk_ref[...]) + _dot(kr.astype(BF16), e_ref[...])
    qt_ref[0, 0] = (qt * scale).astype(BF16)
    k_ref[0] = k.astype(BF16)
    vt_ref[0, 0] = _dot_nt(wvt_ref[...], ckv).astype(BF16)


def _mla_proj(h3, g, w_in_ext, gq, gkv, wuqt, wuqst, wk, wvt, e, cqt, sqt, ck, sk):
    b, s, d = h3.shape
    t = ATTN_TILE
    hk, hv = N_HEADS * LANES, N_HEADS * MLA_V
    gq2, gkv2, g2 = gq.reshape(1, -1), gkv.reshape(1, -1), g.reshape(1, d)
    tab_t = pl.BlockSpec((LANES, t), lambda bi, i: (0, i))
    tab = pl.BlockSpec((t, MLA_ROPE), lambda bi, i: (i, 0))
    return pl.pallas_call(
        functools.partial(_mla_proj_kernel, scale=(MLA_NOPE + MLA_ROPE) ** -0.5 * LOG2E),
        grid=(b, s // t),
        in_specs=[_tok_spec(d), _full(g2), _full(w_in_ext), _full(gq2), _full(gkv2), _full(wuqt), _full(wuqst),
                  _full(wk), _full(wvt), _full(e), tab_t, tab_t, tab, tab],
        out_specs=[_tile_major_spec(hk), _tok_spec(hk), _tile_major_spec(hv)],
        out_shape=[_tile_major_shape(b, s, hk, BF16), jax.ShapeDtypeStruct((b, s, hk), BF16),
                   _tile_major_shape(b, s, hv, BF16)],
        compiler_params=_cparams("parallel", "parallel"),
        name="mla_proj",
    )(h3, g2, w_in_ext, gq2, gkv2, wuqt, wuqst, wk, wvt, e, cqt, sqt, ck, sk)


def _mla_mixer(h, g, w_in, q_norm, w_uq, kv_norm, w_ukv, w_o, b, s):
    n, d = h.shape
    a1 = MLA_Q_RANK + MLA_KV_RANK
    pad = LANES - MLA_NOPE - MLA_ROPE
    swap = _swap_perm(MLA_ROPE, MLA_ROPE, 1)
    bf = lambda w: w.astype(BF16)
    in_cols = -(-(a1 + 2 * MLA_ROPE) // LANES) * LANES
    w_in_ext = bf(jnp.concatenate(
        [w_in, w_in[:, a1:][:, swap], jnp.zeros((d, in_cols - a1 - 2 * MLA_ROPE), w_in.dtype)], axis=1))
    wq = w_uq.reshape(MLA_Q_RANK, N_HEADS, MLA_NOPE + MLA_ROPE)
    zq = lambda w: jnp.zeros((MLA_Q_RANK, N_HEADS, w), w_uq.dtype)
    wuqt = bf(jnp.concatenate([wq, zq(pad)], axis=2).reshape(MLA_Q_RANK, -1).T)
    wuqst = bf(jnp.concatenate([zq(MLA_NOPE), wq[:, :, MLA_NOPE:][:, :, swap], zq(pad)],
                               axis=2).reshape(MLA_Q_RANK, -1).T)
    wkv = w_ukv.reshape(MLA_KV_RANK, N_HEADS, MLA_NOPE + MLA_V)
    wk = bf(jnp.concatenate([wkv[:, :, :MLA_NOPE], jnp.zeros((MLA_KV_RANK, N_HEADS, LANES - MLA_NOPE), w_ukv.dtype)],
                            axis=2).reshape(MLA_KV_RANK, -1))
    wvt = bf(wkv[:, :, MLA_NOPE:].reshape(MLA_KV_RANK, -1).T)
    eye = jnp.broadcast_to(jnp.eye(MLA_ROPE, dtype=BF16)[:, None, :], (MLA_ROPE, N_HEADS, MLA_ROPE))
    ze = lambda w: jnp.zeros((MLA_ROPE, N_HEADS, w), BF16)
    e = jnp.concatenate([ze(MLA_NOPE), eye, ze(pad)], axis=2).reshape(MLA_ROPE, -1)
    ck, sk = _rot_tables(s, MLA_ROPE, 0, MLA_ROPE)
    cq, sq = _rot_tables(s, MLA_ROPE, MLA_NOPE, LANES)
    qt, k, vt = _mla_proj(h.reshape(b, s, d), g, w_in_ext, q_norm, kv_norm, wuqt, wuqst, wk, wvt, e,
                          cq.T, sq.T, ck, sk)
    o = _softmax_attn(qt, k, vt, None, qk_rows=LANES, name="mla_attn")
    return _oproj(h, o.reshape(n, d), w_o)


def kernel(x, norm_ffn1, ffn1_w_gate_up, ffn1_w_down, norm_mix, norm_ffn2, ffn2_w_gate_up, ffn2_w_down,
           sb_w_qkv, sb_w_o, moba_w_qkv, moba_w_o, mla_w_in, mla_q_norm, mla_w_uq, mla_kv_norm,
           mla_w_ukv, mla_w_o, final_norm):
    b, s, d = x.shape
    h = x.reshape(b * s, d)
    bf = lambda w: w.astype(BF16)
    for i in range(DEPTH):
        h = _ffn(h, norm_ffn1[i], bf(ffn1_w_gate_up[i]), bf(ffn1_w_down[i]), final_norm, final=False)
        kind, j = i % 3, i // 3
        if kind == 0:
            h = _sb_mixer(h, norm_mix[i], sb_w_qkv[j], bf(sb_w_o[j]), b, s)
        elif kind == 1:
            h = _moba_mixer(h, norm_mix[i], moba_w_qkv[j], bf(moba_w_o[j]), b, s)
        else:
            h = _mla_mixer(h, norm_mix[i], mla_w_in[j], mla_q_norm[j], mla_w_uq[j], mla_kv_norm[j],
                           mla_w_ukv[j], bf(mla_w_o[j]), b, s)
        h = _ffn(h, norm_ffn2[i], bf(ffn2_w_gate_up[i]), bf(ffn2_w_down[i]), final_norm,
                 final=(i == DEPTH - 1))
    return h.reshape(b, s, d)
```

```python
import functools

import jax
import jax.numpy as jnp
from jax import lax
from jax.experimental import pallas as pl
from jax.experimental.pallas import tpu as pltpu

F32 = jnp.float32
BF16 = jnp.bfloat16

DEPTH = 4
N_HEADS = 16
HEAD_DIM = 64
ROPE_THETA = 500000.0
ROT_DIM = HEAD_DIM // 4
EPS = 1e-6
NEG = -1e30
LOG2E = 1.4426950408889634

MOBA_BLOCK = 256
MOBA_TOPK = 3

MLA_Q_RANK = 384
MLA_KV_RANK = 256
MLA_NOPE = 64
MLA_ROPE = 32
MLA_V = 64

LANES = 128
ATTN_TILE = 256
HEADS_PER_STEP = 8
QUERY_TILES_PER_STEP = 1
VMEM_LIMIT = 48 * 1024 * 1024

_NT = (((1,), (1,)), ((), ()))


def _cparams(*sem):
    return pltpu.CompilerParams(dimension_semantics=sem, vmem_limit_bytes=VMEM_LIMIT)


def _rms(x, g):
    ms = jnp.mean(x * x, axis=-1, keepdims=True)
    return x * lax.rsqrt(ms + EPS) * g


def _dot(a, b):
    return jnp.dot(a, b, preferred_element_type=F32)


def _dot_nt(a, b):
    return lax.dot_general(a, b, _NT, preferred_element_type=F32)


def _ffn_kernel(*refs, final, mixed):
    if mixed:
        h_ref, a_ref, wo_ref, g_ref, wgu_ref, wd_ref, gf_ref, o_ref = refs
        h = h_ref[...] + _dot(a_ref[...], wo_ref[...])
    else:
        h_ref, g_ref, wgu_ref, wd_ref, gf_ref, o_ref = refs
        h = h_ref[...]
    dff = wd_ref.shape[0]
    gu = _dot(_rms(h, g_ref[...]).astype(BF16), wgu_ref[...])
    gate, up = gu[:, :dff], gu[:, dff:]
    act = (gate * jax.nn.sigmoid(gate) * up).astype(BF16)
    y = h + 0.5 * _dot(act, wd_ref[...])
    if final:
        y = _rms(y, gf_ref[...])
    o_ref[...] = y


def _ffn(h, mix, g, w_gate_up, w_down, g_final, *, final, tm=512):
    n, d = h.shape
    assert n % tm == 0
    resident = lambda a: pl.BlockSpec(a.shape, lambda i: (0, 0), pipeline_mode=pl.Buffered(1))
    rows = lambda w: pl.BlockSpec((tm, w), lambda i: (i, 0))
    vec = pl.BlockSpec((1, d), lambda i: (0, 0))
    mix_args = [] if mix is None else list(mix)
    mix_specs = [] if mix is None else [rows(mix[0].shape[1]), resident(mix[1])]
    return pl.pallas_call(
        functools.partial(_ffn_kernel, final=final, mixed=mix is not None),
        grid=(n // tm,),
        in_specs=[rows(d)] + mix_specs + [vec, resident(w_gate_up), resident(w_down), vec],
        out_specs=rows(d),
        out_shape=jax.ShapeDtypeStruct((n, d), F32),
        compiler_params=_cparams("parallel"),
        name="ffn_final" if final else ("ffn_mix" if mix is not None else "ffn"),
    )(h, *mix_args, g.reshape(1, d), w_gate_up, w_down, g_final.reshape(1, d))


def _full(a):
    return pl.BlockSpec(a.shape, lambda bi, i: (0,) * a.ndim)


def _tok_spec(width):
    return pl.BlockSpec((1, ATTN_TILE, width), lambda bi, i: (bi, i, 0))


def _tile_major_spec(rows):
    return pl.BlockSpec((1, 1, rows, ATTN_TILE), lambda bi, i: (bi, i, 0, 0))


def _tile_major_shape(b, s, rows, dtype):
    return jax.ShapeDtypeStruct((b, s // ATTN_TILE, rows, ATTN_TILE), dtype)


def _split_pair_t(x2):
    r = lax.broadcasted_iota(jnp.int32, x2.shape, 0)
    xf = x2.astype(F32)
    return (jnp.where(r < HEAD_DIM, xf, 0.0).astype(BF16),
            jnp.where(r < HEAD_DIM, 0.0, xf).astype(BF16))


def _rows128(ref, lead, p):
    return ref[lead + (slice(p * LANES, (p + 1) * LANES), slice(None))]


def _keys(k_ref, rows, p):
    return k_ref[0, rows, p * LANES:(p + 1) * LANES]


def _values_t(vt_ref, j, h):
    return vt_ref[0, j, h * HEAD_DIM:(h + 1) * HEAD_DIM, :]


def _store_heads_t(o_ref, outs):
    for p in range(len(outs) // 2):
        o2 = jnp.concatenate([outs[2 * p], outs[2 * p + 1]], axis=0)
        o_ref[0, :, p * LANES:(p + 1) * LANES] = o2.T.astype(o_ref.dtype)


def _sb_proj_kernel(h_ref, g_ref, wqt_ref, wk_ref, wvt_ref, qt_ref, k_ref, vt_ref, *, scale):
    xn = _rms(h_ref[0], g_ref[...]).astype(BF16)
    qt_ref[0, 0] = (_dot_nt(wqt_ref[...], xn) * scale).astype(BF16)
    k_ref[0] = _dot(xn, wk_ref[...]).astype(BF16)
    vt_ref[0, 0] = _dot_nt(wvt_ref[...], xn).astype(BF16)


def _sb_proj(h3, g, wqt, wk, wvt):
    b, s, d = h3.shape
    g2 = g.reshape(1, d)
    return pl.pallas_call(
        functools.partial(_sb_proj_kernel, scale=HEAD_DIM ** -0.5),
        grid=(b, s // ATTN_TILE),
        in_specs=[_tok_spec(d), _full(g2), _full(wqt), _full(wk), _full(wvt)],
        out_specs=[_tile_major_spec(d), _tok_spec(d), _tile_major_spec(d)],
        out_shape=[_tile_major_shape(b, s, d, BF16), jax.ShapeDtypeStruct((b, s, d), BF16),
                   _tile_major_shape(b, s, d, BF16)],
        compiler_params=_cparams("parallel", "parallel"),
        name="sb_proj",
    )(h3, g2, wqt, wk, wvt)


def _sb_attn_kernel(qt_ref, k_ref, vt_ref, o_ref, c_ref, acc_ref, *, t, g, nq):
    i = pl.program_id(2)
    tq = nq * t
    kk = lax.broadcasted_iota(jnp.int32, (t, t), 0)
    kj = lax.broadcasted_iota(jnp.int32, (t, t), 1)
    tri = jnp.where(kj > kk, 1.0, 0.0).astype(BF16)
    tri2 = jnp.concatenate([tri, tri], axis=1)
    key = lax.broadcasted_iota(jnp.int32, (t, tq), 0)
    qry = lax.broadcasted_iota(jnp.int32, (t, tq), 1)
    qts = _query_operands(qt_ref, g, nq, HEAD_DIM)

    def tile(j, a):
        rows = pl.ds(pl.multiple_of(j * t, t), t)
        hs = range(g)
        zs = [_dot(_keys(k_ref, rows, h // 2), qts[h]) for h in hs]
        sps = [jnp.log(1.0 + jnp.exp2(jnp.abs(z) * -LOG2E)) for z in zs]
        lbs = [jnp.minimum(zs[h], 0.0) - sps[h] for h in hs]
        lks = [lbs[h] - zs[h] for h in hs]
        if a is not None:
            past = key + a * t < qry
            lks = [jnp.where(past, lk, 0.0) for lk in lks]
        his = [lk.astype(BF16) for lk in lks]
        los = [(lks[h] - his[h].astype(F32)).astype(BF16) for h in hs]
        tls = [_dot(tri2, jnp.concatenate([his[h], los[h]], axis=0)) for h in hs]
        ws = [jnp.exp(lbs[h] + tls[h]) for h in hs]
        if a is not None:
            ws = [jnp.where(past, w, 0.0) for w in ws]
        for h in hs:
            c = c_ref[h]
            acc_ref[h] += jnp.exp(c) * _dot(_values_t(vt_ref, j, h), ws[h].astype(BF16))
            c_ref[h] = c + (tls[h][:1, :] + lks[h][:1, :])

    c_ref[...] = jnp.zeros(c_ref.shape, F32)
    acc_ref[...] = jnp.zeros(acc_ref.shape, F32)
    for a in reversed(range(nq)):
        tile(i * nq + a, a)

    @pl.loop(0, i * nq)
    def _(s):
        tile(i * nq - 1 - s, None)

    _store_heads_t(o_ref, [acc_ref[h] for h in range(g)])


def _query_operands(qt_ref, g, nq, qk_rows):
    wide = lambda p: jnp.concatenate([_rows128(qt_ref, (0, a), p) for a in range(nq)], axis=1)
    if qk_rows == LANES:
        return [wide(h) for h in range(g)]
    qts = []
    for p in range(g // 2):
        qts.extend(_split_pair_t(wide(p)))
    return qts


def _attn_specs(s, t, g, nq, qk_rows):
    wq, wv = g * qk_rows, g * HEAD_DIM
    nt = s // t
    return ([pl.BlockSpec((1, nq, wq, t), lambda bi, hi, i: (bi, i, hi, 0)),
             pl.BlockSpec((1, s, wq), lambda bi, hi, i: (bi, 0, hi)),
             pl.BlockSpec((1, nt, wv, t), lambda bi, hi, i: (bi, 0, hi, 0))],
            pl.BlockSpec((1, nq * t, wv), lambda bi, hi, i: (bi, i, hi)))


def _sb_attn(qt, k, vt, *, g=HEADS_PER_STEP, nq=QUERY_TILES_PER_STEP):
    b, s, d = k.shape
    t = ATTN_TILE
    in_specs, out_spec = _attn_specs(s, t, g, nq, HEAD_DIM)
    return pl.pallas_call(
        functools.partial(_sb_attn_kernel, t=t, g=g, nq=nq),
        grid=(b, N_HEADS // g, s // (nq * t)),
        in_specs=in_specs,
        out_specs=out_spec,
        out_shape=jax.ShapeDtypeStruct((b, s, d), BF16),
        scratch_shapes=[pltpu.VMEM((g, 1, nq * t), F32), pltpu.VMEM((g, HEAD_DIM, nq * t), F32)],
        compiler_params=_cparams("parallel", "parallel", "arbitrary"),
        name="sb_attn",
    )(qt, k, vt)


def _sb_mixer(h, g, w_qkv, b, s):
    n, d = h.shape
    wqt, wk, wvt = w_qkv[:, :d].T.astype(BF16), w_qkv[:, d:2 * d].astype(BF16), w_qkv[:, 2 * d:].T.astype(BF16)
    qt, k, vt = _sb_proj(h.reshape(b, s, d), g, wqt, wk, wvt)
    return _sb_attn(qt, k, vt).reshape(n, d)


def _softmax_attn_kernel(*refs, t, g, nq, qk_rows, has_sel):
    if has_sel:
        qt_ref, k_ref, vt_ref, sel_ref, o_ref, m_ref, l_ref, acc_ref = refs
    else:
        qt_ref, k_ref, vt_ref, o_ref, m_ref, l_ref, acc_ref = refs
    i = pl.program_id(2)
    tq = nq * t
    key = lax.broadcasted_iota(jnp.int32, (t, tq), 0)
    qry = lax.broadcasted_iota(jnp.int32, (t, tq), 1)
    qts = _query_operands(qt_ref, g, nq, qk_rows)
    if qk_rows == HEAD_DIM:
        k_of = lambda rows, h: _keys(k_ref, rows, h // 2)
    else:
        k_of = lambda rows, h: _keys(k_ref, rows, h)

    def tile(j, a):
        rows = pl.ds(pl.multiple_of(j * t, t), t)
        hs = range(g)
        scs = [_dot(k_of(rows, h), qts[h]) for h in hs]
        sel = sel_ref[0, j] if has_sel else None
        if a is not None:
            causal = key + a * t <= qry
            if has_sel:
                own = jnp.where(causal, 1.0, 0.0)
                vis = [jnp.where(qry < (a + 1) * t, own, sel[h:h + 1, :]) > 0.5 for h in hs]
            else:
                vis = [causal] * g
            scs = [jnp.where(vis[h], scs[h], NEG) for h in hs]
        elif has_sel:
            scs = [jnp.where(sel[h:h + 1, :] > 0.5, scs[h], NEG) for h in hs]
        ms = [m_ref[h] for h in hs]
        mns = [jnp.maximum(ms[h], jnp.max(scs[h], axis=0, keepdims=True)) for h in hs]
        alphas = [jnp.exp2(ms[h] - mns[h]) for h in hs]
        ps = [jnp.exp2(scs[h] - mns[h]) for h in hs]
        for h in hs:
            m_ref[h] = mns[h]
            l_ref[h] = alphas[h] * l_ref[h] + jnp.sum(ps[h], axis=0, keepdims=True)
        for h in hs:
            acc_ref[h] = alphas[h] * acc_ref[h] + _dot(_values_t(vt_ref, j, h), ps[h].astype(BF16))

    m_ref[...] = jnp.full(m_ref.shape, NEG, F32)
    l_ref[...] = jnp.zeros(l_ref.shape, F32)
    acc_ref[...] = jnp.zeros(acc_ref.shape, F32)
    for a in range(nq):
        tile(i * nq + a, a)

    @pl.loop(0, i * nq)
    def _(j):
        tile(j, None)

    _store_heads_t(o_ref, [acc_ref[h] / l_ref[h] for h in range(g)])


def _softmax_attn(qt, k, vt, sel, *, qk_rows, name, g=HEADS_PER_STEP, nq=QUERY_TILES_PER_STEP):
    b, s, _ = k.shape
    t = ATTN_TILE
    in_specs, out_spec = _attn_specs(s, t, g, nq, qk_rows)
    args = [qt, k, vt]
    if sel is not None:
        in_specs.append(pl.BlockSpec((1, sel.shape[1], g, nq * t), lambda bi, hi, i: (bi, 0, hi, i)))
        args.append(sel)
    return pl.pallas_call(
        functools.partial(_softmax_attn_kernel, t=t, g=g, nq=nq, qk_rows=qk_rows, has_sel=sel is not None),
        grid=(b, N_HEADS // g, s // (nq * t)),
        in_specs=in_specs,
        out_specs=out_spec,
        out_shape=jax.ShapeDtypeStruct((b, s, N_HEADS * HEAD_DIM), BF16),
        scratch_shapes=[pltpu.VMEM((g, 1, nq * t), F32), pltpu.VMEM((g, 1, nq * t), F32),
                        pltpu.VMEM((g, HEAD_DIM, nq * t), F32)],
        compiler_params=_cparams("parallel", "parallel", "arbitrary"),
        name=name,
    )(*args)


def _rot_tables(s, r, lead, width):
    inv = ROPE_THETA ** (-jnp.arange(0, r, 2, dtype=F32) / r)
    ang = jnp.arange(s).astype(F32)[:, None] * inv[None, :]
    cos, sin = jnp.cos(ang), jnp.sin(ang)
    one = lambda w: jnp.ones((s, w), F32)
    zero = lambda w: jnp.zeros((s, w), F32)
    rest = width - lead - r
    return (jnp.concatenate([one(lead), cos, cos, one(rest)], axis=-1),
            jnp.concatenate([zero(lead), -sin, sin, zero(rest)], axis=-1))


def _swap_perm(r, width, n):
    d = jnp.arange(width)
    p = jnp.where(d < r // 2, d + r // 2, jnp.where(d < r, d - r // 2, d))
    return (jnp.arange(n)[:, None] * width + p[None, :]).reshape(-1)


def _moba_proj_kernel(h_ref, g_ref, wqt_ref, wqst_ref, wk_ref, wks_ref, wvt_ref,
                      cost_ref, sint_ref, cos_ref, sin_ref,
                      qt_ref, qft_ref, k_ref, vt_ref, km_ref, *, scale):
    reps = h_ref.shape[-1] // LANES
    xn = _rms(h_ref[0], g_ref[...]).astype(BF16)
    cost, sint = jnp.tile(cost_ref[...], (reps, 1)), jnp.tile(sint_ref[...], (reps, 1))
    cos, sin = jnp.tile(cos_ref[...], (1, reps)), jnp.tile(sin_ref[...], (1, reps))
    qt = _dot_nt(wqt_ref[...], xn) * cost + _dot_nt(wqst_ref[...], xn) * sint
    k = _dot(xn, wk_ref[...]) * cos + _dot(xn, wks_ref[...]) * sin
    qt_ref[0, 0] = (qt * scale).astype(BF16)
    qft_ref[0, 0] = qt
    k_ref[0] = k.astype(BF16)
    vt_ref[0, 0] = _dot_nt(wvt_ref[...], xn).astype(BF16)
    km_ref[0, 0] = jnp.mean(k, axis=0, keepdims=True)


def _moba_proj(h3, g, wqt, wqst, wk, wks, wvt, cost, sint, cos, sin):
    b, s, d = h3.shape
    t = ATTN_TILE
    g2 = g.reshape(1, d)
    tab_t = pl.BlockSpec((LANES, t), lambda bi, i: (0, i))
    tab = pl.BlockSpec((t, LANES), lambda bi, i: (i, 0))
    return pl.pallas_call(
        functools.partial(_moba_proj_kernel, scale=HEAD_DIM ** -0.5 * LOG2E),
        grid=(b, s // t),
        in_specs=[_tok_spec(d), _full(g2), _full(wqt), _full(wqst), _full(wk), _full(wks), _full(wvt),
                  tab_t, tab_t, tab, tab],
        out_specs=[_tile_major_spec(d), _tile_major_spec(d), _tok_spec(d), _tile_major_spec(d),
                   pl.BlockSpec((1, 1, 1, d), lambda bi, i: (bi, i, 0, 0))],
        out_shape=[_tile_major_shape(b, s, d, BF16), _tile_major_shape(b, s, d, F32),
                   jax.ShapeDtypeStruct((b, s, d), BF16), _tile_major_shape(b, s, d, BF16),
                   jax.ShapeDtypeStruct((b, s // t, 1, d), F32)],
        compiler_params=_cparams("parallel", "parallel"),
        name="moba_proj",
    )(h3, g2, wqt, wqst, wk, wks, wvt, cost, sint, cos, sin)


def _moba_select_kernel(qt_ref, km_ref, sel_ref, *, nb):
    t = qt_ref.shape[-1]
    t0 = pl.program_id(1) * t
    qt = qt_ref[0, 0]
    km = km_ref[0]
    r = lax.broadcasted_iota(jnp.int32, km.shape, 0)
    c = lax.broadcasted_iota(jnp.int32, km.shape, 1)
    km = jnp.where((r % N_HEADS) == (c // HEAD_DIM), km, 0.0)
    qh = qt.astype(BF16)
    ql = (qt - qh.astype(F32)).astype(BF16)
    kh = km.astype(BF16)
    kl = (km - kh.astype(F32)).astype(BF16)
    gate = _dot(kh, qh) + _dot(kh, ql) + _dot(kl, qh)
    qblk = (t0 + lax.broadcasted_iota(jnp.int32, (N_HEADS, t), 1)) // MOBA_BLOCK
    gs = [jnp.where(n < qblk, gate[n * N_HEADS:(n + 1) * N_HEADS, :], NEG) for n in range(nb)]
    for n in range(nb):
        rank = jnp.zeros((N_HEADS, t), F32)
        for m in range(nb):
            if m == n:
                continue
            beats = (gs[m] >= gs[n]) if m < n else (gs[m] > gs[n])
            rank = rank + jnp.where(beats, 1.0, 0.0)
        keep = jnp.where(rank < MOBA_TOPK - 0.5, 1.0, 0.0)
        sel_ref[0, n] = jnp.where(n < qblk, keep, 0.0)


def _moba_select(qft, km_rep):
    b, nt, d, t = qft.shape
    s = nt * t
    nb = s // MOBA_BLOCK
    return pl.pallas_call(
        functools.partial(_moba_select_kernel, nb=nb),
        grid=(b, nt),
        in_specs=[_tile_major_spec(d), pl.BlockSpec((1, nb * N_HEADS, d), lambda bi, i: (bi, 0, 0))],
        out_specs=pl.BlockSpec((1, nb, N_HEADS, t), lambda bi, i: (bi, 0, 0, i)),
        out_shape=jax.ShapeDtypeStruct((b, nb, N_HEADS, s), F32),
        compiler_params=_cparams("parallel", "parallel"),
        name="moba_select",
    )(qft, km_rep)


def _moba_mixer(h, g, w_qkv, b, s):
    n, d = h.shape
    assert s % MOBA_BLOCK == 0 and MOBA_BLOCK == ATTN_TILE
    nb = s // MOBA_BLOCK
    perm = _swap_perm(ROT_DIM, HEAD_DIM, N_HEADS)
    wq, wk, wv = w_qkv[:, :d], w_qkv[:, d:2 * d], w_qkv[:, 2 * d:]
    bf = lambda w: w.astype(BF16)
    cos, sin = (jnp.tile(x, (1, LANES // HEAD_DIM)) for x in _rot_tables(s, ROT_DIM, 0, HEAD_DIM))
    qt, qft, k, vt, km = _moba_proj(h.reshape(b, s, d), g, bf(wq.T), bf(wq[:, perm].T), bf(wk), bf(wk[:, perm]),
                                    bf(wv.T), cos.T, sin.T, cos, sin)
    km_rep = jnp.repeat(km.reshape(b, nb, d), N_HEADS, axis=1)
    sel = _moba_select(qft, km_rep)
    return _softmax_attn(qt, k, vt, sel, qk_rows=HEAD_DIM, name="moba_attn").reshape(n, d)


def _mla_proj_kernel(h_ref, g_ref, win_ref, gq_ref, gkv_ref, wuqt_ref, wuqst_ref, wk_ref, wvt_ref, e_ref,
                     cqt_ref, sqt_ref, ck_ref, sk_ref, qt_ref, k_ref, vt_ref, *, scale):
    xn = _rms(h_ref[0], g_ref[...]).astype(BF16)
    c = _dot(xn, win_ref[...])
    a0, a1, a2 = MLA_Q_RANK, MLA_Q_RANK + MLA_KV_RANK, MLA_Q_RANK + MLA_KV_RANK + MLA_ROPE
    cq = _rms(c[:, :a0], gq_ref[...]).astype(BF16)
    ckv = _rms(c[:, a0:a1], gkv_ref[...]).astype(BF16)
    kr = c[:, a1:a2] * ck_ref[...] + c[:, a2:a2 + MLA_ROPE] * sk_ref[...]
    cost, sint = jnp.tile(cqt_ref[...], (N_HEADS, 1)), jnp.tile(sqt_ref[...], (N_HEADS, 1))
    qt = _dot_nt(wuqt_ref[...], cq) * cost + _dot_nt(wuqst_ref[...], cq) * sint
    k = _dot(ckv, wk_ref[...]) + _dot(kr.astype(BF16), e_ref[...])
    qt_ref[0, 0] = (qt * scale).astype(BF16)
    k_ref[0] = k.astype(BF16)
    vt_ref[0, 0] = _dot_nt(wvt_ref[...], ckv).astype(BF16)


def _mla_proj(h3, g, w_in_ext, gq, gkv, wuqt, wuqst, wk, wvt, e, cqt, sqt, ck, sk):
    b, s, d = h3.shape
    t = ATTN_TILE
    hk, hv = N_HEADS * LANES, N_HEADS * MLA_V
    gq2, gkv2, g2 = gq.reshape(1, -1), gkv.reshape(1, -1), g.reshape(1, d)
    tab_t = pl.BlockSpec((LANES, t), lambda bi, i: (0, i))
    tab = pl.BlockSpec((t, MLA_ROPE), lambda bi, i: (i, 0))
    return pl.pallas_call(
        functools.partial(_mla_proj_kernel, scale=(MLA_NOPE + MLA_ROPE) ** -0.5 * LOG2E),
        grid=(b, s // t),
        in_specs=[_tok_spec(d), _full(g2), _full(w_in_ext), _full(gq2), _full(gkv2), _full(wuqt), _full(wuqst),
                  _full(wk), _full(wvt), _full(e), tab_t, tab_t, tab, tab],
        out_specs=[_tile_major_spec(hk), _tok_spec(hk), _tile_major_spec(hv)],
        out_shape=[_tile_major_shape(b, s, hk, BF16), jax.ShapeDtypeStruct((b, s, hk), BF16),
                   _tile_major_shape(b, s, hv, BF16)],
        compiler_params=_cparams("parallel", "parallel"),
        name="mla_proj",
    )(h3, g2, w_in_ext, gq2, gkv2, wuqt, wuqst, wk, wvt, e, cqt, sqt, ck, sk)


def _mla_mixer(h, g, w_in, q_norm, w_uq, kv_norm, w_ukv, b, s):
    n, d = h.shape
    a1 = MLA_Q_RANK + MLA_KV_RANK
    pad = LANES - MLA_NOPE - MLA_ROPE
    swap = _swap_perm(MLA_ROPE, MLA_ROPE, 1)
    bf = lambda w: w.astype(BF16)
    in_cols = -(-(a1 + 2 * MLA_ROPE) // LANES) * LANES
    w_in_ext = bf(jnp.concatenate(
        [w_in, w_in[:, a1:][:, swap], jnp.zeros((d, in_cols - a1 - 2 * MLA_ROPE), w_in.dtype)], axis=1))
    wq = w_uq.reshape(MLA_Q_RANK, N_HEADS, MLA_NOPE + MLA_ROPE)
    zq = lambda w: jnp.zeros((MLA_Q_RANK, N_HEADS, w), w_uq.dtype)
    wuqt = bf(jnp.concatenate([wq, zq(pad)], axis=2).reshape(MLA_Q_RANK, -1).T)
    wuqst = bf(jnp.concatenate([zq(MLA_NOPE), wq[:, :, MLA_NOPE:][:, :, swap], zq(pad)],
                               axis=2).reshape(MLA_Q_RANK, -1).T)
    wkv = w_ukv.reshape(MLA_KV_RANK, N_HEADS, MLA_NOPE + MLA_V)
    wk = bf(jnp.concatenate([wkv[:, :, :MLA_NOPE], jnp.zeros((MLA_KV_RANK, N_HEADS, LANES - MLA_NOPE), w_ukv.dtype)],
                            axis=2).reshape(MLA_KV_RANK, -1))
    wvt = bf(wkv[:, :, MLA_NOPE:].reshape(MLA_KV_RANK, -1).T)
    eye = jnp.broadcast_to(jnp.eye(MLA_ROPE, dtype=BF16)[:, None, :], (MLA_ROPE, N_HEADS, MLA_ROPE))
    ze = lambda w: jnp.zeros((MLA_ROPE, N_HEADS, w), BF16)
    e = jnp.concatenate([ze(MLA_NOPE), eye, ze(pad)], axis=2).reshape(MLA_ROPE, -1)
    ck, sk = _rot_tables(s, MLA_ROPE, 0, MLA_ROPE)
    cq, sq = _rot_tables(s, MLA_ROPE, MLA_NOPE, LANES)
    qt, k, vt = _mla_proj(h.reshape(b, s, d), g, w_in_ext, q_norm, kv_norm, wuqt, wuqst, wk, wvt, e,
                          cq.T, sq.T, ck, sk)
    return _softmax_attn(qt, k, vt, None, qk_rows=LANES, name="mla_attn").reshape(n, N_HEADS * MLA_V)


def kernel(x, norm_ffn1, ffn1_w_gate_up, ffn1_w_down, norm_mix, norm_ffn2, ffn2_w_gate_up, ffn2_w_down,
           sb_w_qkv, sb_w_o, moba_w_qkv, moba_w_o, mla_w_in, mla_q_norm, mla_w_uq, mla_kv_norm,
           mla_w_ukv, mla_w_o, final_norm):
    b, s, d = x.shape
    h = x.reshape(b * s, d)
    bf = lambda w: w.astype(BF16)
    for i in range(DEPTH):
        h = _ffn(h, None, norm_ffn1[i], bf(ffn1_w_gate_up[i]), bf(ffn1_w_down[i]), final_norm, final=False)
        kind, j = i % 3, i // 3
        if kind == 0:
            mix = (_sb_mixer(h, norm_mix[i], sb_w_qkv[j], b, s), bf(sb_w_o[j]))
        elif kind == 1:
            mix = (_moba_mixer(h, norm_mix[i], moba_w_qkv[j], b, s), bf(moba_w_o[j]))
        else:
            mix = (_mla_mixer(h, norm_mix[i], mla_w_in[j], mla_q_norm[j], mla_w_uq[j], mla_kv_norm[j],
                              mla_w_ukv[j], b, s), bf(mla_w_o[j]))
        h = _ffn(h, mix, norm_ffn2[i], bf(ffn2_w_gate_up[i]), bf(ffn2_w_down[i]), final_norm,
                 final=(i == DEPTH - 1))
    return h.reshape(b, s, d)
```

```python
import functools

import jax
import jax.numpy as jnp
from jax import lax
from jax.experimental import pallas as pl
from jax.experimental.pallas import tpu as pltpu

F32 = jnp.float32
BF16 = jnp.bfloat16

DEPTH = 4
N_HEADS = 16
HEAD_DIM = 64
ROPE_THETA = 500000.0
ROT_DIM = HEAD_DIM // 4
EPS = 1e-6
NEG = -1e30
LOG2E = 1.4426950408889634

MOBA_BLOCK = 256
MOBA_TOPK = 3

MLA_Q_RANK = 384
MLA_KV_RANK = 256
MLA_NOPE = 64
MLA_ROPE = 32
MLA_V = 64

LANES = 128
ATTN_TILE = 256
HEADS_PER_STEP = 8
QUERY_TILES_PER_STEP = 1
VMEM_LIMIT = 48 * 1024 * 1024

_NT = (((1,), (1,)), ((), ()))


def _cparams(*sem):
    return pltpu.CompilerParams(dimension_semantics=sem, vmem_limit_bytes=VMEM_LIMIT)


def _rms(x, g):
    ms = jnp.mean(x * x, axis=-1, keepdims=True)
    return x * lax.rsqrt(ms + EPS) * g


def _dot(a, b):
    return jnp.dot(a, b, preferred_element_type=F32)


def _dot_nt(a, b):
    return lax.dot_general(a, b, _NT, preferred_element_type=F32)


def _ffn_kernel(*refs, final, mixed):
    if mixed:
        h_ref, a_ref, wo_ref, g_ref, wgu_ref, wd_ref, gf_ref, o_ref = refs
        h = h_ref[...] + _dot(a_ref[...], wo_ref[...])
    else:
        h_ref, g_ref, wgu_ref, wd_ref, gf_ref, o_ref = refs
        h = h_ref[...]
    dff = wd_ref.shape[0]
    gu = _dot(_rms(h, g_ref[...]).astype(BF16), wgu_ref[...])
    gate, up = gu[:, :dff], gu[:, dff:]
    act = (gate * jax.nn.sigmoid(gate) * up).astype(BF16)
    y = h + 0.5 * _dot(act, wd_ref[...])
    if final:
        y = _rms(y, gf_ref[...])
    o_ref[...] = y


def _ffn(h, mix, g, w_gate_up, w_down, g_final, *, final, tm=512):
    n, d = h.shape
    assert n % tm == 0
    resident = lambda a: pl.BlockSpec(a.shape, lambda i: (0, 0), pipeline_mode=pl.Buffered(1))
    rows = lambda w: pl.BlockSpec((tm, w), lambda i: (i, 0))
    vec = pl.BlockSpec((1, d), lambda i: (0, 0))
    mix_args = [] if mix is None else list(mix)
    mix_specs = [] if mix is None else [rows(mix[0].shape[1]), resident(mix[1])]
    return pl.pallas_call(
        functools.partial(_ffn_kernel, final=final, mixed=mix is not None),
        grid=(n // tm,),
        in_specs=[rows(d)] + mix_specs + [vec, resident(w_gate_up), resident(w_down), vec],
        out_specs=rows(d),
        out_shape=jax.ShapeDtypeStruct((n, d), F32),
        compiler_params=_cparams("parallel"),
        name="ffn_final" if final else ("ffn_mix" if mix is not None else "ffn"),
    )(h, *mix_args, g.reshape(1, d), w_gate_up, w_down, g_final.reshape(1, d))


def _full(a):
    return pl.BlockSpec(a.shape, lambda bi, i: (0,) * a.ndim)


def _tok_spec(width):
    return pl.BlockSpec((1, ATTN_TILE, width), lambda bi, i: (bi, i, 0))


def _tile_major_spec(rows):
    return pl.BlockSpec((1, 1, rows, ATTN_TILE), lambda bi, i: (bi, i, 0, 0))


def _tile_major_shape(b, s, rows, dtype):
    return jax.ShapeDtypeStruct((b, s // ATTN_TILE, rows, ATTN_TILE), dtype)


def _split_pair_t(x2):
    r = lax.broadcasted_iota(jnp.int32, x2.shape, 0)
    xf = x2.astype(F32)
    return (jnp.where(r < HEAD_DIM, xf, 0.0).astype(BF16),
            jnp.where(r < HEAD_DIM, 0.0, xf).astype(BF16))


def _rows128(ref, lead, p):
    return ref[lead + (slice(p * LANES, (p + 1) * LANES), slice(None))]


def _keys(k_ref, rows, p):
    return k_ref[0, rows, p * LANES:(p + 1) * LANES]


def _values_t(vt_ref, j, h):
    return vt_ref[0, j, h * HEAD_DIM:(h + 1) * HEAD_DIM, :]


def _store_heads_t(o_ref, outs):
    for p in range(len(outs) // 2):
        o2 = jnp.concatenate([outs[2 * p], outs[2 * p + 1]], axis=0)
        o_ref[0, :, p * LANES:(p + 1) * LANES] = o2.T.astype(o_ref.dtype)


def _sb_proj_kernel(h_ref, g_ref, wqt_ref, wk_ref, wvt_ref, qt_ref, k_ref, vt_ref, *, scale):
    xn = _rms(h_ref[0], g_ref[...]).astype(BF16)
    qt_ref[0, 0] = (_dot_nt(wqt_ref[...], xn) * scale).astype(BF16)
    k_ref[0] = _dot(xn, wk_ref[...]).astype(BF16)
    vt_ref[0, 0] = _dot_nt(wvt_ref[...], xn).astype(BF16)


def _sb_proj(h3, g, wqt, wk, wvt):
    b, s, d = h3.shape
    g2 = g.reshape(1, d)
    return pl.pallas_call(
        functools.partial(_sb_proj_kernel, scale=HEAD_DIM ** -0.5),
        grid=(b, s // ATTN_TILE),
        in_specs=[_tok_spec(d), _full(g2), _full(wqt), _full(wk), _full(wvt)],
        out_specs=[_tile_major_spec(d), _tok_spec(d), _tile_major_spec(d)],
        out_shape=[_tile_major_shape(b, s, d, BF16), jax.ShapeDtypeStruct((b, s, d), BF16),
                   _tile_major_shape(b, s, d, BF16)],
        compiler_params=_cparams("parallel", "parallel"),
        name="sb_proj",
    )(h3, g2, wqt, wk, wvt)


def _sb_attn_kernel(qt_ref, k_ref, vt_ref, o_ref, c_ref, acc_ref, *, t, g, nq):
    i = pl.program_id(2)
    tq = nq * t
    half = t // 2
    kk = lax.broadcasted_iota(jnp.int32, (half, half), 0)
    kj = lax.broadcasted_iota(jnp.int32, (half, half), 1)
    tri = jnp.where(kj >= kk, 1.0, 0.0).astype(BF16)
    tri2 = jnp.concatenate([tri, tri], axis=1)

    def inclusive_sums(hi, lo):
        upper = _dot(tri2, jnp.concatenate([hi[half:], lo[half:]], axis=0))
        lower = _dot(tri2, jnp.concatenate([hi[:half], lo[:half]], axis=0)) + upper[:1, :]
        return jnp.concatenate([lower, upper], axis=0)
    key = lax.broadcasted_iota(jnp.int32, (t, tq), 0)
    qry = lax.broadcasted_iota(jnp.int32, (t, tq), 1)
    qts = _query_operands(qt_ref, g, nq, HEAD_DIM)

    def tile(j, a):
        rows = pl.ds(pl.multiple_of(j * t, t), t)
        hs = range(g)
        zs = [_dot(_keys(k_ref, rows, h // 2), qts[h]) for h in hs]
        sps = [jnp.maximum(z, 0.0) + jnp.log(1.0 + jnp.exp2(jnp.abs(z) * -LOG2E)) for z in zs]
        if a is not None:
            past = key + a * t < qry
            sps = [jnp.where(past, sp, 0.0) for sp in sps]
        his = [sp.astype(BF16) for sp in sps]
        los = [(sps[h] - his[h].astype(F32)).astype(BF16) for h in hs]
        sums = [inclusive_sums(his[h], los[h]) for h in hs]
        ws = [jnp.exp(zs[h] - sums[h]) for h in hs]
        if a is not None:
            ws = [jnp.where(past, w, 0.0) for w in ws]
        for h in hs:
            c = c_ref[h]
            acc_ref[h] += jnp.exp(-c) * _dot(_values_t(vt_ref, j, h), ws[h].astype(BF16))
            c_ref[h] = c + sums[h][:1, :]

    c_ref[...] = jnp.zeros(c_ref.shape, F32)
    acc_ref[...] = jnp.zeros(acc_ref.shape, F32)
    for a in reversed(range(nq)):
        tile(i * nq + a, a)

    @pl.loop(0, i * nq)
    def _(s):
        tile(i * nq - 1 - s, None)

    _store_heads_t(o_ref, [acc_ref[h] for h in range(g)])


def _query_operands(qt_ref, g, nq, qk_rows):
    wide = lambda p: jnp.concatenate([_rows128(qt_ref, (0, a), p) for a in range(nq)], axis=1)
    if qk_rows == LANES:
        return [wide(h) for h in range(g)]
    qts = []
    for p in range(g // 2):
        qts.extend(_split_pair_t(wide(p)))
    return qts


def _attn_specs(s, t, g, nq, qk_rows):
    wq, wv = g * qk_rows, g * HEAD_DIM
    nt = s // t
    return ([pl.BlockSpec((1, nq, wq, t), lambda bi, hi, i: (bi, i, hi, 0)),
             pl.BlockSpec((1, s, wq), lambda bi, hi, i: (bi, 0, hi)),
             pl.BlockSpec((1, nt, wv, t), lambda bi, hi, i: (bi, 0, hi, 0))],
            pl.BlockSpec((1, nq * t, wv), lambda bi, hi, i: (bi, i, hi)))


def _sb_attn(qt, k, vt, *, g=HEADS_PER_STEP, nq=QUERY_TILES_PER_STEP):
    b, s, d = k.shape
    t = ATTN_TILE
    in_specs, out_spec = _attn_specs(s, t, g, nq, HEAD_DIM)
    return pl.pallas_call(
        functools.partial(_sb_attn_kernel, t=t, g=g, nq=nq),
        grid=(b, N_HEADS // g, s // (nq * t)),
        in_specs=in_specs,
        out_specs=out_spec,
        out_shape=jax.ShapeDtypeStruct((b, s, d), BF16),
        scratch_shapes=[pltpu.VMEM((g, 1, nq * t), F32), pltpu.VMEM((g, HEAD_DIM, nq * t), F32)],
        compiler_params=_cparams("parallel", "parallel", "arbitrary"),
        name="sb_attn",
    )(qt, k, vt)


def _sb_mixer(h, g, w_qkv, b, s):
    n, d = h.shape
    wqt, wk, wvt = w_qkv[:, :d].T.astype(BF16), w_qkv[:, d:2 * d].astype(BF16), w_qkv[:, 2 * d:].T.astype(BF16)
    qt, k, vt = _sb_proj(h.reshape(b, s, d), g, wqt, wk, wvt)
    return _sb_attn(qt, k, vt).reshape(n, d)


def _softmax_attn_kernel(*refs, t, g, nq, qk_rows, has_sel):
    if has_sel:
        qt_ref, k_ref, vt_ref, sel_ref, o_ref, m_ref, l_ref, acc_ref = refs
    else:
        qt_ref, k_ref, vt_ref, o_ref, m_ref, l_ref, acc_ref = refs
    i = pl.program_id(2)
    tq = nq * t
    key = lax.broadcasted_iota(jnp.int32, (t, tq), 0)
    qry = lax.broadcasted_iota(jnp.int32, (t, tq), 1)
    qts = _query_operands(qt_ref, g, nq, qk_rows)
    if qk_rows == HEAD_DIM:
        k_of = lambda rows, h: _keys(k_ref, rows, h // 2)
    else:
        k_of = lambda rows, h: _keys(k_ref, rows, h)

    def tile(j, a):
        rows = pl.ds(pl.multiple_of(j * t, t), t)
        hs = range(g)
        scs = [_dot(k_of(rows, h), qts[h]) for h in hs]
        sel = sel_ref[0, j] if has_sel else None
        if a is not None:
            causal = key + a * t <= qry
            if has_sel:
                own = jnp.where(causal, 1.0, 0.0)
                vis = [jnp.where(qry < (a + 1) * t, own, sel[h:h + 1, :]) > 0.5 for h in hs]
            else:
                vis = [causal] * g
            scs = [jnp.where(vis[h], scs[h], NEG) for h in hs]
        elif has_sel:
            scs = [jnp.where(sel[h:h + 1, :] > 0.5, scs[h], NEG) for h in hs]
        ms = [m_ref[h] for h in hs]
        mns = [jnp.maximum(ms[h], jnp.max(scs[h], axis=0, keepdims=True)) for h in hs]
        alphas = [jnp.exp2(ms[h] - mns[h]) for h in hs]
        ps = [jnp.exp2(scs[h] - mns[h]) for h in hs]
        for h in hs:
            m_ref[h] = mns[h]
            l_ref[h] = alphas[h] * l_ref[h] + jnp.sum(ps[h], axis=0, keepdims=True)
        for h in hs:
            acc_ref[h] = alphas[h] * acc_ref[h] + _dot(_values_t(vt_ref, j, h), ps[h].astype(BF16))

    m_ref[...] = jnp.full(m_ref.shape, NEG, F32)
    l_ref[...] = jnp.zeros(l_ref.shape, F32)
    acc_ref[...] = jnp.zeros(acc_ref.shape, F32)
    for a in range(nq):
        tile(i * nq + a, a)

    @pl.loop(0, i * nq)
    def _(j):
        tile(j, None)

    _store_heads_t(o_ref, [acc_ref[h] / l_ref[h] for h in range(g)])


def _softmax_attn(qt, k, vt, sel, *, qk_rows, name, g=HEADS_PER_STEP, nq=QUERY_TILES_PER_STEP):
    b, s, _ = k.shape
    t = ATTN_TILE
    in_specs, out_spec = _attn_specs(s, t, g, nq, qk_rows)
    args = [qt, k, vt]
    if sel is not None:
        in_specs.append(pl.BlockSpec((1, sel.shape[1], g, nq * t), lambda bi, hi, i: (bi, 0, hi, i)))
        args.append(sel)
    return pl.pallas_call(
        functools.partial(_softmax_attn_kernel, t=t, g=g, nq=nq, qk_rows=qk_rows, has_sel=sel is not None),
        grid=(b, N_HEADS // g, s // (nq * t)),
        in_specs=in_specs,
        out_specs=out_spec,
        out_shape=jax.ShapeDtypeStruct((b, s, N_HEADS * HEAD_DIM), BF16),
        scratch_shapes=[pltpu.VMEM((g, 1, nq * t), F32), pltpu.VMEM((g, 1, nq * t), F32),
                        pltpu.VMEM((g, HEAD_DIM, nq * t), F32)],
        compiler_params=_cparams("parallel", "parallel", "arbitrary"),
        name=name,
    )(*args)


def _rot_tables(s, r, lead, width):
    inv = ROPE_THETA ** (-jnp.arange(0, r, 2, dtype=F32) / r)
    ang = jnp.arange(s).astype(F32)[:, None] * inv[None, :]
    cos, sin = jnp.cos(ang), jnp.sin(ang)
    one = lambda w: jnp.ones((s, w), F32)
    zero = lambda w: jnp.zeros((s, w), F32)
    rest = width - lead - r
    return (jnp.concatenate([one(lead), cos, cos, one(rest)], axis=-1),
            jnp.concatenate([zero(lead), -sin, sin, zero(rest)], axis=-1))


def _swap_perm(r, width, n):
    d = jnp.arange(width)
    p = jnp.where(d < r // 2, d + r // 2, jnp.where(d < r, d - r // 2, d))
    return (jnp.arange(n)[:, None] * width + p[None, :]).reshape(-1)


def _moba_proj_kernel(h_ref, g_ref, wqt_ref, wqst_ref, wk_ref, wks_ref, wvt_ref,
                      cost_ref, sint_ref, cos_ref, sin_ref,
                      qt_ref, qft_ref, k_ref, vt_ref, km_ref, *, scale):
    reps = h_ref.shape[-1] // LANES
    xn = _rms(h_ref[0], g_ref[...]).astype(BF16)
    cost, sint = jnp.tile(cost_ref[...], (reps, 1)), jnp.tile(sint_ref[...], (reps, 1))
    cos, sin = jnp.tile(cos_ref[...], (1, reps)), jnp.tile(sin_ref[...], (1, reps))
    qt = _dot_nt(wqt_ref[...], xn) * cost + _dot_nt(wqst_ref[...], xn) * sint
    k = _dot(xn, wk_ref[...]) * cos + _dot(xn, wks_ref[...]) * sin
    qt_ref[0, 0] = (qt * scale).astype(BF16)
    qft_ref[0, 0] = qt
    k_ref[0] = k.astype(BF16)
    vt_ref[0, 0] = _dot_nt(wvt_ref[...], xn).astype(BF16)
    km_ref[0, 0] = jnp.mean(k, axis=0, keepdims=True)


def _moba_proj(h3, g, wqt, wqst, wk, wks, wvt, cost, sint, cos, sin):
    b, s, d = h3.shape
    t = ATTN_TILE
    g2 = g.reshape(1, d)
    tab_t = pl.BlockSpec((LANES, t), lambda bi, i: (0, i))
    tab = pl.BlockSpec((t, LANES), lambda bi, i: (i, 0))
    return pl.pallas_call(
        functools.partial(_moba_proj_kernel, scale=HEAD_DIM ** -0.5 * LOG2E),
        grid=(b, s // t),
        in_specs=[_tok_spec(d), _full(g2), _full(wqt), _full(wqst), _full(wk), _full(wks), _full(wvt),
                  tab_t, tab_t, tab, tab],
        out_specs=[_tile_major_spec(d), _tile_major_spec(d), _tok_spec(d), _tile_major_spec(d),
                   pl.BlockSpec((1, 1, 1, d), lambda bi, i: (bi, i, 0, 0))],
        out_shape=[_tile_major_shape(b, s, d, BF16), _tile_major_shape(b, s, d, F32),
                   jax.ShapeDtypeStruct((b, s, d), BF16), _tile_major_shape(b, s, d, BF16),
                   jax.ShapeDtypeStruct((b, s // t, 1, d), F32)],
        compiler_params=_cparams("parallel", "parallel"),
        name="moba_proj",
    )(h3, g2, wqt, wqst, wk, wks, wvt, cost, sint, cos, sin)


def _moba_select_kernel(qt_ref, km_ref, sel_ref, *, nb):
    t = qt_ref.shape[-1]
    t0 = pl.program_id(1) * t
    qt = qt_ref[0, 0]
    km = km_ref[0]
    r = lax.broadcasted_iota(jnp.int32, km.shape, 0)
    c = lax.broadcasted_iota(jnp.int32, km.shape, 1)
    km = jnp.where((r % N_HEADS) == (c // HEAD_DIM), km, 0.0)
    qh = qt.astype(BF16)
    ql = (qt - qh.astype(F32)).astype(BF16)
    kh = km.astype(BF16)
    kl = (km - kh.astype(F32)).astype(BF16)
    gate = _dot(kh, qh) + _dot(kh, ql) + _dot(kl, qh)
    qblk = (t0 + lax.broadcasted_iota(jnp.int32, (N_HEADS, t), 1)) // MOBA_BLOCK
    gs = [jnp.where(n < qblk, gate[n * N_HEADS:(n + 1) * N_HEADS, :], NEG) for n in range(nb)]
    for n in range(nb):
        rank = jnp.zeros((N_HEADS, t), F32)
        for m in range(nb):
            if m == n:
                continue
            beats = (gs[m] >= gs[n]) if m < n else (gs[m] > gs[n])
            rank = rank + jnp.where(beats, 1.0, 0.0)
        keep = jnp.where(rank < MOBA_TOPK - 0.5, 1.0, 0.0)
        sel_ref[0, n] = jnp.where(n < qblk, keep, 0.0)


def _moba_select(qft, km_rep):
    b, nt, d, t = qft.shape
    s = nt * t
    nb = s // MOBA_BLOCK
    return pl.pallas_call(
        functools.partial(_moba_select_kernel, nb=nb),
        grid=(b, nt),
        in_specs=[_tile_major_spec(d), pl.BlockSpec((1, nb * N_HEADS, d), lambda bi, i: (bi, 0, 0))],
        out_specs=pl.BlockSpec((1, nb, N_HEADS, t), lambda bi, i: (bi, 0, 0, i)),
        out_shape=jax.ShapeDtypeStruct((b, nb, N_HEADS, s), F32),
        compiler_params=_cparams("parallel", "parallel"),
        name="moba_select",
    )(qft, km_rep)


def _moba_mixer(h, g, w_qkv, b, s):
    n, d = h.shape
    assert s % MOBA_BLOCK == 0 and MOBA_BLOCK == ATTN_TILE
    nb = s // MOBA_BLOCK
    perm = _swap_perm(ROT_DIM, HEAD_DIM, N_HEADS)
    wq, wk, wv = w_qkv[:, :d], w_qkv[:, d:2 * d], w_qkv[:, 2 * d:]
    bf = lambda w: w.astype(BF16)
    cos, sin = (jnp.tile(x, (1, LANES // HEAD_DIM)) for x in _rot_tables(s, ROT_DIM, 0, HEAD_DIM))
    qt, qft, k, vt, km = _moba_proj(h.reshape(b, s, d), g, bf(wq.T), bf(wq[:, perm].T), bf(wk), bf(wk[:, perm]),
                                    bf(wv.T), cos.T, sin.T, cos, sin)
    km_rep = jnp.repeat(km.reshape(b, nb, d), N_HEADS, axis=1)
    sel = _moba_select(qft, km_rep)
    return _softmax_attn(qt, k, vt, sel, qk_rows=HEAD_DIM, name="moba_attn").reshape(n, d)


def _mla_proj_kernel(h_ref, g_ref, win_ref, gq_ref, gkv_ref, wuqt_ref, wuqst_ref, wk_ref, wvt_ref, e_ref,
                     cqt_ref, sqt_ref, ck_ref, sk_ref, qt_ref, k_ref, vt_ref, *, scale):
    xn = _rms(h_ref[0], g_ref[...]).astype(BF16)
    c = _dot(xn, win_ref[...])
    a0, a1, a2 = MLA_Q_RANK, MLA_Q_RANK + MLA_KV_RANK, MLA_Q_RANK + MLA_KV_RANK + MLA_ROPE
    cq = _rms(c[:, :a0], gq_ref[...]).astype(BF16)
    ckv = _rms(c[:, a0:a1], gkv_ref[...]).astype(BF16)
    kr = c[:, a1:a2] * ck_ref[...] + c[:, a2:a2 + MLA_ROPE] * sk_ref[...]
    cost, sint = jnp.tile(cqt_ref[...], (N_HEADS, 1)), jnp.tile(sqt_ref[...], (N_HEADS, 1))
    qt = _dot_nt(wuqt_ref[...], cq) * cost + _dot_nt(wuqst_ref[...], cq) * sint
    k = _dot(ckv, wk_ref[...]) + _dot(kr.astype(BF16), e_ref[...])
    qt_ref[0, 0] = (qt * scale).astype(BF16)
    k_ref[0] = k.astype(BF16)
    vt_ref[0, 0] = _dot_nt(wvt_ref[...], ckv).astype(BF16)


def _mla_proj(h3, g, w_in_ext, gq, gkv, wuqt, wuqst, wk, wvt, e, cqt, sqt, ck, sk):
    b, s, d = h3.shape
    t = ATTN_TILE
    hk, hv = N_HEADS * LANES, N_HEADS * MLA_V
    gq2, gkv2, g2 = gq.reshape(1, -1), gkv.reshape(1, -1), g.reshape(1, d)
    tab_t = pl.BlockSpec((LANES, t), lambda bi, i: (0, i))
    tab = pl.BlockSpec((t, MLA_ROPE), lambda bi, i: (i, 0))
    return pl.pallas_call(
        functools.partial(_mla_proj_kernel, scale=(MLA_NOPE + MLA_ROPE) ** -0.5 * LOG2E),
        grid=(b, s // t),
        in_specs=[_tok_spec(d), _full(g2), _full(w_in_ext), _full(gq2), _full(gkv2), _full(wuqt), _full(wuqst),
                  _full(wk), _full(wvt), _full(e), tab_t, tab_t, tab, tab],
        out_specs=[_tile_major_spec(hk), _tok_spec(hk), _tile_major_spec(hv)],
        out_shape=[_tile_major_shape(b, s, hk, BF16), jax.ShapeDtypeStruct((b, s, hk), BF16),
                   _tile_major_shape(b, s, hv, BF16)],
        compiler_params=_cparams("parallel", "parallel"),
        name="mla_proj",
    )(h3, g2, w_in_ext, gq2, gkv2, wuqt, wuqst, wk, wvt, e, cqt, sqt, ck, sk)


def _mla_mixer(h, g, w_in, q_norm, w_uq, kv_norm, w_ukv, b, s):
    n, d = h.shape
    a1 = MLA_Q_RANK + MLA_KV_RANK
    pad = LANES - MLA_NOPE - MLA_ROPE
    swap = _swap_perm(MLA_ROPE, MLA_ROPE, 1)
    bf = lambda w: w.astype(BF16)
    in_cols = -(-(a1 + 2 * MLA_ROPE) // LANES) * LANES
    w_in_ext = bf(jnp.concatenate(
        [w_in, w_in[:, a1:][:, swap], jnp.zeros((d, in_cols - a1 - 2 * MLA_ROPE), w_in.dtype)], axis=1))
    wq = w_uq.reshape(MLA_Q_RANK, N_HEADS, MLA_NOPE + MLA_ROPE)
    zq = lambda w: jnp.zeros((MLA_Q_RANK, N_HEADS, w), w_uq.dtype)
    wuqt = bf(jnp.concatenate([wq, zq(pad)], axis=2).reshape(MLA_Q_RANK, -1).T)
    wuqst = bf(jnp.concatenate([zq(MLA_NOPE), wq[:, :, MLA_NOPE:][:, :, swap], zq(pad)],
                               axis=2).reshape(MLA_Q_RANK, -1).T)
    wkv = w_ukv.reshape(MLA_KV_RANK, N_HEADS, MLA_NOPE + MLA_V)
    wk = bf(jnp.concatenate([wkv[:, :, :MLA_NOPE], jnp.zeros((MLA_KV_RANK, N_HEADS, LANES - MLA_NOPE), w_ukv.dtype)],
                            axis=2).reshape(MLA_KV_RANK, -1))
    wvt = bf(wkv[:, :, MLA_NOPE:].reshape(MLA_KV_RANK, -1).T)
    eye = jnp.broadcast_to(jnp.eye(MLA_ROPE, dtype=BF16)[:, None, :], (MLA_ROPE, N_HEADS, MLA_ROPE))
    ze = lambda w: jnp.zeros((MLA_ROPE, N_HEADS, w), BF16)
    e = jnp.concatenate([ze(MLA_NOPE), eye, ze(pad)], axis=2).reshape(MLA_ROPE, -1)
    ck, sk = _rot_tables(s, MLA_ROPE, 0, MLA_ROPE)
    cq, sq = _rot_tables(s, MLA_ROPE, MLA_NOPE, LANES)
    qt, k, vt = _mla_proj(h.reshape(b, s, d), g, w_in_ext, q_norm, kv_norm, wuqt, wuqst, wk, wvt, e,
                          cq.T, sq.T, ck, sk)
    return _softmax_attn(qt, k, vt, None, qk_rows=LANES, name="mla_attn").reshape(n, N_HEADS * MLA_V)


def kernel(x, norm_ffn1, ffn1_w_gate_up, ffn1_w_down, norm_mix, norm_ffn2, ffn2_w_gate_up, ffn2_w_down,
           sb_w_qkv, sb_w_o, moba_w_qkv, moba_w_o, mla_w_in, mla_q_norm, mla_w_uq, mla_kv_norm,
           mla_w_ukv, mla_w_o, final_norm):
    b, s, d = x.shape
    h = x.reshape(b * s, d)
    bf = lambda w: w.astype(BF16)
    for i in range(DEPTH):
        h = _ffn(h, None, norm_ffn1[i], bf(ffn1_w_gate_up[i]), bf(ffn1_w_down[i]), final_norm, final=False)
        kind, j = i % 3, i // 3
        if kind == 0:
            mix = (_sb_mixer(h, norm_mix[i], sb_w_qkv[j], b, s), bf(sb_w_o[j]))
        elif kind == 1:
            mix = (_moba_mixer(h, norm_mix[i], moba_w_qkv[j], b, s), bf(moba_w_o[j]))
        else:
            mix = (_mla_mixer(h, norm_mix[i], mla_w_in[j], mla_q_norm[j], mla_w_uq[j], mla_kv_norm[j],
                              mla_w_ukv[j], b, s), bf(mla_w_o[j]))
        h = _ffn(h, mix, norm_ffn2[i], bf(ffn2_w_gate_up[i]), bf(ffn2_w_down[i]), final_norm,
                 final=(i == DEPTH - 1))
    return h.reshape(b, s, d)
```

```python
import functools

import jax
import jax.numpy as jnp
from jax import lax
from jax.experimental import pallas as pl
from jax.experimental.pallas import tpu as pltpu

F32 = jnp.float32
BF16 = jnp.bfloat16

DEPTH = 4
N_HEADS = 16
HEAD_DIM = 64
ROPE_THETA = 500000.0
ROT_DIM = HEAD_DIM // 4
EPS = 1e-6
NEG = -1e30
LOG2E = 1.4426950408889634

MOBA_BLOCK = 256
MOBA_TOPK = 3

MLA_Q_RANK = 384
MLA_KV_RANK = 256
MLA_NOPE = 64
MLA_ROPE = 32
MLA_V = 64

LANES = 128
ATTN_TILE = 256
HEADS_PER_STEP = 8
QUERY_TILES_PER_STEP = 1
SUM_ROWS = 16
VMEM_LIMIT = 48 * 1024 * 1024

_NT = (((1,), (1,)), ((), ()))


def _cparams(*sem):
    return pltpu.CompilerParams(dimension_semantics=sem, vmem_limit_bytes=VMEM_LIMIT)


def _rms(x, g):
    ms = jnp.mean(x * x, axis=-1, keepdims=True)
    return x * lax.rsqrt(ms + EPS) * g


def _dot(a, b):
    return jnp.dot(a, b, preferred_element_type=F32)


def _dot_nt(a, b):
    return lax.dot_general(a, b, _NT, preferred_element_type=F32)


def _ffn_kernel(*refs, final, mixed):
    if mixed:
        h_ref, a_ref, wo_ref, g_ref, wgu_ref, wd_ref, gf_ref, o_ref = refs
        h = h_ref[...] + _dot(a_ref[...], wo_ref[...])
    else:
        h_ref, g_ref, wgu_ref, wd_ref, gf_ref, o_ref = refs
        h = h_ref[...]
    dff = wd_ref.shape[0]
    gu = _dot(_rms(h, g_ref[...]).astype(BF16), wgu_ref[...])
    gate, up = gu[:, :dff], gu[:, dff:]
    act = (gate * jax.nn.sigmoid(gate) * up).astype(BF16)
    y = h + 0.5 * _dot(act, wd_ref[...])
    if final:
        y = _rms(y, gf_ref[...])
    o_ref[...] = y


def _ffn(h, mix, g, w_gate_up, w_down, g_final, *, final, tm=512):
    n, d = h.shape
    assert n % tm == 0
    resident = lambda a: pl.BlockSpec(a.shape, lambda i: (0, 0), pipeline_mode=pl.Buffered(1))
    rows = lambda w: pl.BlockSpec((tm, w), lambda i: (i, 0))
    vec = pl.BlockSpec((1, d), lambda i: (0, 0))
    mix_args = [] if mix is None else list(mix)
    mix_specs = [] if mix is None else [rows(mix[0].shape[1]), resident(mix[1])]
    return pl.pallas_call(
        functools.partial(_ffn_kernel, final=final, mixed=mix is not None),
        grid=(n // tm,),
        in_specs=[rows(d)] + mix_specs + [vec, resident(w_gate_up), resident(w_down), vec],
        out_specs=rows(d),
        out_shape=jax.ShapeDtypeStruct((n, d), F32),
        compiler_params=_cparams("parallel"),
        name="ffn_final" if final else ("ffn_mix" if mix is not None else "ffn"),
    )(h, *mix_args, g.reshape(1, d), w_gate_up, w_down, g_final.reshape(1, d))


def _full(a):
    return pl.BlockSpec(a.shape, lambda bi, i: (0,) * a.ndim)


def _tok_spec(width):
    return pl.BlockSpec((1, ATTN_TILE, width), lambda bi, i: (bi, i, 0))


def _tile_major_spec(rows):
    return pl.BlockSpec((1, 1, rows, ATTN_TILE), lambda bi, i: (bi, i, 0, 0))


def _tile_major_shape(b, s, rows, dtype):
    return jax.ShapeDtypeStruct((b, s // ATTN_TILE, rows, ATTN_TILE), dtype)


def _split_pair_t(x2):
    r = lax.broadcasted_iota(jnp.int32, x2.shape, 0)
    xf = x2.astype(F32)
    return (jnp.where(r < HEAD_DIM, xf, 0.0).astype(BF16),
            jnp.where(r < HEAD_DIM, 0.0, xf).astype(BF16))


def _rows128(ref, lead, p):
    return ref[lead + (slice(p * LANES, (p + 1) * LANES), slice(None))]


def _keys(k_ref, rows, p):
    return k_ref[0, rows, p * LANES:(p + 1) * LANES]


def _values_t(vt_ref, j, h):
    return vt_ref[0, j, h * HEAD_DIM:(h + 1) * HEAD_DIM, :]


def _store_heads_t(o_ref, outs):
    for p in range(len(outs) // 2):
        o2 = jnp.concatenate([outs[2 * p], outs[2 * p + 1]], axis=0)
        o_ref[0, :, p * LANES:(p + 1) * LANES] = o2.T.astype(o_ref.dtype)


def _sb_proj_kernel(h_ref, g_ref, wqt_ref, wk_ref, wvt_ref, qt_ref, k_ref, vt_ref, *, scale):
    xn = _rms(h_ref[0], g_ref[...]).astype(BF16)
    qt_ref[0, 0] = (_dot_nt(wqt_ref[...], xn) * scale).astype(BF16)
    k_ref[0] = _dot(xn, wk_ref[...]).astype(BF16)
    vt_ref[0, 0] = _dot_nt(wvt_ref[...], xn).astype(BF16)


def _sb_proj(h3, g, wqt, wk, wvt):
    b, s, d = h3.shape
    g2 = g.reshape(1, d)
    return pl.pallas_call(
        functools.partial(_sb_proj_kernel, scale=HEAD_DIM ** -0.5),
        grid=(b, s // ATTN_TILE),
        in_specs=[_tok_spec(d), _full(g2), _full(wqt), _full(wk), _full(wvt)],
        out_specs=[_tile_major_spec(d), _tok_spec(d), _tile_major_spec(d)],
        out_shape=[_tile_major_shape(b, s, d, BF16), jax.ShapeDtypeStruct((b, s, d), BF16),
                   _tile_major_shape(b, s, d, BF16)],
        compiler_params=_cparams("parallel", "parallel"),
        name="sb_proj",
    )(h3, g2, wqt, wk, wvt)


def _sb_attn_kernel(qt_ref, k_ref, vt_ref, o_ref, c_ref, acc_ref, *, t, g, nq):
    i = pl.program_id(2)
    tq = nq * t
    half = t // 2
    kk = lax.broadcasted_iota(jnp.int32, (half, half), 0)
    kj = lax.broadcasted_iota(jnp.int32, (half, half), 1)
    tri = jnp.where(kj >= kk, 1.0, 0.0).astype(BF16)
    tri2 = jnp.concatenate([tri, tri], axis=1)

    def inclusive_sums(hi, lo):
        upper = _dot(tri2, jnp.concatenate([hi[half:], lo[half:]], axis=0))
        lower = _dot(tri2, jnp.concatenate([hi[:half], lo[:half]], axis=0)) + upper[:1, :]
        return jnp.concatenate([lower, upper], axis=0)
    key = lax.broadcasted_iota(jnp.int32, (t, tq), 0)
    qry = lax.broadcasted_iota(jnp.int32, (t, tq), 1)
    qts = _query_operands(qt_ref, g, nq, HEAD_DIM)

    def tile(j, a):
        rows = pl.ds(pl.multiple_of(j * t, t), t)
        hs = range(g)
        zs = [_dot(_keys(k_ref, rows, h // 2), qts[h]) for h in hs]
        sps = [jnp.maximum(z, 0.0) + jnp.log(1.0 + jnp.exp2(jnp.abs(z) * -LOG2E)) for z in zs]
        if a is not None:
            past = key + a * t < qry
            sps = [jnp.where(past, sp, 0.0) for sp in sps]
        his = [sp.astype(BF16) for sp in sps]
        los = [(sps[h] - his[h].astype(F32)).astype(BF16) for h in hs]
        sums = [inclusive_sums(his[h], los[h]) for h in hs]
        ws = [jnp.exp(zs[h] - sums[h]) for h in hs]
        if a is not None:
            ws = [jnp.where(past, w, 0.0) for w in ws]
        for h in hs:
            c = c_ref[h]
            acc_ref[h] += jnp.exp(-c) * _dot(_values_t(vt_ref, j, h), ws[h].astype(BF16))
            c_ref[h] = c + sums[h][:1, :]

    c_ref[...] = jnp.zeros(c_ref.shape, F32)
    acc_ref[...] = jnp.zeros(acc_ref.shape, F32)
    for a in reversed(range(nq)):
        tile(i * nq + a, a)

    @pl.loop(0, i * nq)
    def _(s):
        tile(i * nq - 1 - s, None)

    _store_heads_t(o_ref, [acc_ref[h] for h in range(g)])


def _query_operands(qt_ref, g, nq, qk_rows):
    wide = lambda p: jnp.concatenate([_rows128(qt_ref, (0, a), p) for a in range(nq)], axis=1)
    if qk_rows == LANES:
        return [wide(h) for h in range(g)]
    qts = []
    for p in range(g // 2):
        qts.extend(_split_pair_t(wide(p)))
    return qts


def _attn_specs(s, t, g, nq, qk_rows):
    wq, wv = g * qk_rows, g * HEAD_DIM
    nt = s // t
    return ([pl.BlockSpec((1, nq, wq, t), lambda bi, hi, i: (bi, i, hi, 0)),
             pl.BlockSpec((1, s, wq), lambda bi, hi, i: (bi, 0, hi)),
             pl.BlockSpec((1, nt, wv, t), lambda bi, hi, i: (bi, 0, hi, 0))],
            pl.BlockSpec((1, nq * t, wv), lambda bi, hi, i: (bi, i, hi)))


def _sb_attn(qt, k, vt, *, g=HEADS_PER_STEP, nq=QUERY_TILES_PER_STEP):
    b, s, d = k.shape
    t = ATTN_TILE
    in_specs, out_spec = _attn_specs(s, t, g, nq, HEAD_DIM)
    return pl.pallas_call(
        functools.partial(_sb_attn_kernel, t=t, g=g, nq=nq),
        grid=(b, N_HEADS // g, s // (nq * t)),
        in_specs=in_specs,
        out_specs=out_spec,
        out_shape=jax.ShapeDtypeStruct((b, s, d), BF16),
        scratch_shapes=[pltpu.VMEM((g, 1, nq * t), F32), pltpu.VMEM((g, HEAD_DIM, nq * t), F32)],
        compiler_params=_cparams("parallel", "parallel", "arbitrary"),
        name="sb_attn",
    )(qt, k, vt)


def _sb_mixer(h, g, w_qkv, b, s):
    n, d = h.shape
    wqt, wk, wvt = w_qkv[:, :d].T.astype(BF16), w_qkv[:, d:2 * d].astype(BF16), w_qkv[:, 2 * d:].T.astype(BF16)
    qt, k, vt = _sb_proj(h.reshape(b, s, d), g, wqt, wk, wvt)
    return _sb_attn(qt, k, vt).reshape(n, d)


def _softmax_attn_kernel(*refs, t, g, nq, qk_rows, has_sel):
    if has_sel:
        qt_ref, k_ref, vt_ref, sel_ref, o_ref, m_ref, acc_ref = refs
    else:
        qt_ref, k_ref, vt_ref, o_ref, m_ref, acc_ref = refs
    i = pl.program_id(2)
    tq = nq * t
    key = lax.broadcasted_iota(jnp.int32, (t, tq), 0)
    qry = lax.broadcasted_iota(jnp.int32, (t, tq), 1)
    qts = _query_operands(qt_ref, g, nq, qk_rows)
    if qk_rows == HEAD_DIM:
        k_of = lambda rows, h: _keys(k_ref, rows, h // 2)
    else:
        k_of = lambda rows, h: _keys(k_ref, rows, h)
    ones = jnp.ones((SUM_ROWS, t), BF16)

    def tile(j, a):
        rows = pl.ds(pl.multiple_of(j * t, t), t)
        hs = range(g)
        scs = [_dot(k_of(rows, h), qts[h]) for h in hs]
        sel = sel_ref[0, j] if has_sel else None
        if a is not None:
            causal = key + a * t <= qry
            if has_sel:
                own = jnp.where(causal, 1.0, 0.0)
                vis = [jnp.where(qry < (a + 1) * t, own, sel[h:h + 1, :]) > 0.5 for h in hs]
            else:
                vis = [causal] * g
            scs = [jnp.where(vis[h], scs[h], NEG) for h in hs]
        elif has_sel:
            scs = [jnp.where(sel[h:h + 1, :] > 0.5, scs[h], NEG) for h in hs]
        ms = [m_ref[h] for h in hs]
        mns = [jnp.maximum(ms[h], jnp.max(scs[h], axis=0, keepdims=True)) for h in hs]
        alphas = [jnp.exp2(ms[h] - mns[h]) for h in hs]
        ps = [jnp.exp2(scs[h] - mns[h]) for h in hs]
        for h in hs:
            m_ref[h] = mns[h]
        for h in hs:
            vext = jnp.concatenate([_values_t(vt_ref, j, h), ones], axis=0)
            acc_ref[h] = alphas[h] * acc_ref[h] + _dot(vext, ps[h].astype(BF16))

    m_ref[...] = jnp.full(m_ref.shape, NEG, F32)
    acc_ref[...] = jnp.zeros(acc_ref.shape, F32)
    for a in range(nq):
        tile(i * nq + a, a)

    @pl.loop(0, i * nq)
    def _(j):
        tile(j, None)

    _store_heads_t(o_ref, [acc_ref[h, :HEAD_DIM] / acc_ref[h, HEAD_DIM:HEAD_DIM + 1] for h in range(g)])


def _softmax_attn(qt, k, vt, sel, *, qk_rows, name, g=HEADS_PER_STEP, nq=QUERY_TILES_PER_STEP):
    b, s, _ = k.shape
    t = ATTN_TILE
    in_specs, out_spec = _attn_specs(s, t, g, nq, qk_rows)
    args = [qt, k, vt]
    if sel is not None:
        in_specs.append(pl.BlockSpec((1, sel.shape[1], g, nq * t), lambda bi, hi, i: (bi, 0, hi, i)))
        args.append(sel)
    return pl.pallas_call(
        functools.partial(_softmax_attn_kernel, t=t, g=g, nq=nq, qk_rows=qk_rows, has_sel=sel is not None),
        grid=(b, N_HEADS // g, s // (nq * t)),
        in_specs=in_specs,
        out_specs=out_spec,
        out_shape=jax.ShapeDtypeStruct((b, s, N_HEADS * HEAD_DIM), BF16),
        scratch_shapes=[pltpu.VMEM((g, 1, nq * t), F32), pltpu.VMEM((g, HEAD_DIM + SUM_ROWS, nq * t), F32)],
        compiler_params=_cparams("parallel", "parallel", "arbitrary"),
        name=name,
    )(*args)


def _rot_tables(s, r, lead, width):
    inv = ROPE_THETA ** (-jnp.arange(0, r, 2, dtype=F32) / r)
    ang = jnp.arange(s).astype(F32)[:, None] * inv[None, :]
    cos, sin = jnp.cos(ang), jnp.sin(ang)
    one = lambda w: jnp.ones((s, w), F32)
    zero = lambda w: jnp.zeros((s, w), F32)
    rest = width - lead - r
    return (jnp.concatenate([one(lead), cos, cos, one(rest)], axis=-1),
            jnp.concatenate([zero(lead), -sin, sin, zero(rest)], axis=-1))


def _swap_perm(r, width, n):
    d = jnp.arange(width)
    p = jnp.where(d < r // 2, d + r // 2, jnp.where(d < r, d - r // 2, d))
    return (jnp.arange(n)[:, None] * width + p[None, :]).reshape(-1)


def _swap_halves_t(xt, lead, r, width):
    half = r // 2
    x4 = xt.reshape(xt.shape[0] // width, width // half, half, xt.shape[1])
    a = lead // half
    parts = [x4[:, :a], x4[:, a + 1:a + 2], x4[:, a:a + 1], x4[:, a + 2:]]
    return jnp.concatenate([p for p in parts if p.shape[1]], axis=1).reshape(xt.shape)


def _moba_proj_kernel(h_ref, g_ref, wqt_ref, wk_ref, wks_ref, wvt_ref,
                      cost_ref, sint_ref, cos_ref, sin_ref,
                      qt_ref, qft_ref, k_ref, vt_ref, km_ref, *, scale):
    reps = h_ref.shape[-1] // LANES
    xn = _rms(h_ref[0], g_ref[...]).astype(BF16)
    cost, sint = jnp.tile(cost_ref[...], (reps, 1)), jnp.tile(sint_ref[...], (reps, 1))
    cos, sin = jnp.tile(cos_ref[...], (1, reps)), jnp.tile(sin_ref[...], (1, reps))
    qt = _dot_nt(wqt_ref[...], xn)
    qt = qt * cost + _swap_halves_t(qt, 0, ROT_DIM, HEAD_DIM) * sint
    k = _dot(xn, wk_ref[...]) * cos + _dot(xn, wks_ref[...]) * sin
    qt_ref[0, 0] = (qt * scale).astype(BF16)
    qft_ref[0, 0] = qt
    k_ref[0] = k.astype(BF16)
    vt_ref[0, 0] = _dot_nt(wvt_ref[...], xn).astype(BF16)
    km_ref[0, 0] = jnp.mean(k, axis=0, keepdims=True)


def _moba_proj(h3, g, wqt, wk, wks, wvt, cost, sint, cos, sin):
    b, s, d = h3.shape
    t = ATTN_TILE
    g2 = g.reshape(1, d)
    tab_t = pl.BlockSpec((LANES, t), lambda bi, i: (0, i))
    tab = pl.BlockSpec((t, LANES), lambda bi, i: (i, 0))
    return pl.pallas_call(
        functools.partial(_moba_proj_kernel, scale=HEAD_DIM ** -0.5 * LOG2E),
        grid=(b, s // t),
        in_specs=[_tok_spec(d), _full(g2), _full(wqt), _full(wk), _full(wks), _full(wvt),
                  tab_t, tab_t, tab, tab],
        out_specs=[_tile_major_spec(d), _tile_major_spec(d), _tok_spec(d), _tile_major_spec(d),
                   pl.BlockSpec((1, 1, 1, d), lambda bi, i: (bi, i, 0, 0))],
        out_shape=[_tile_major_shape(b, s, d, BF16), _tile_major_shape(b, s, d, F32),
                   jax.ShapeDtypeStruct((b, s, d), BF16), _tile_major_shape(b, s, d, BF16),
                   jax.ShapeDtypeStruct((b, s // t, 1, d), F32)],
        compiler_params=_cparams("parallel", "parallel"),
        name="moba_proj",
    )(h3, g2, wqt, wk, wks, wvt, cost, sint, cos, sin)


def _moba_select_kernel(qt_ref, km_ref, sel_ref, *, nb):
    t = qt_ref.shape[-1]
    t0 = pl.program_id(1) * t
    qt = qt_ref[0, 0]
    km = km_ref[0]
    r = lax.broadcasted_iota(jnp.int32, km.shape, 0)
    c = lax.broadcasted_iota(jnp.int32, km.shape, 1)
    km = jnp.where((r % N_HEADS) == (c // HEAD_DIM), km, 0.0)
    qh = qt.astype(BF16)
    ql = (qt - qh.astype(F32)).astype(BF16)
    kh = km.astype(BF16)
    kl = (km - kh.astype(F32)).astype(BF16)
    gate = _dot(kh, qh) + _dot(kh, ql) + _dot(kl, qh)
    qblk = (t0 + lax.broadcasted_iota(jnp.int32, (N_HEADS, t), 1)) // MOBA_BLOCK
    gs = [jnp.where(n < qblk, gate[n * N_HEADS:(n + 1) * N_HEADS, :], NEG) for n in range(nb)]
    for n in range(nb):
        rank = jnp.zeros((N_HEADS, t), F32)
        for m in range(nb):
            if m == n:
                continue
            beats = (gs[m] >= gs[n]) if m < n else (gs[m] > gs[n])
            rank = rank + jnp.where(beats, 1.0, 0.0)
        keep = jnp.where(rank < MOBA_TOPK - 0.5, 1.0, 0.0)
        sel_ref[0, n] = jnp.where(n < qblk, keep, 0.0)


def _moba_select(qft, km_rep):
    b, nt, d, t = qft.shape
    s = nt * t
    nb = s // MOBA_BLOCK
    return pl.pallas_call(
        functools.partial(_moba_select_kernel, nb=nb),
        grid=(b, nt),
        in_specs=[_tile_major_spec(d), pl.BlockSpec((1, nb * N_HEADS, d), lambda bi, i: (bi, 0, 0))],
        out_specs=pl.BlockSpec((1, nb, N_HEADS, t), lambda bi, i: (bi, 0, 0, i)),
        out_shape=jax.ShapeDtypeStruct((b, nb, N_HEADS, s), F32),
        compiler_params=_cparams("parallel", "parallel"),
        name="moba_select",
    )(qft, km_rep)


def _moba_mixer(h, g, w_qkv, b, s):
    n, d = h.shape
    assert s % MOBA_BLOCK == 0 and MOBA_BLOCK == ATTN_TILE
    nb = s // MOBA_BLOCK
    perm = _swap_perm(ROT_DIM, HEAD_DIM, N_HEADS)
    wq, wk, wv = w_qkv[:, :d], w_qkv[:, d:2 * d], w_qkv[:, 2 * d:]
    bf = lambda w: w.astype(BF16)
    cos, sin = (jnp.tile(x, (1, LANES // HEAD_DIM)) for x in _rot_tables(s, ROT_DIM, 0, HEAD_DIM))
    qt, qft, k, vt, km = _moba_proj(h.reshape(b, s, d), g, bf(wq.T), bf(wk), bf(wk[:, perm]),
                                    bf(wv.T), cos.T, sin.T, cos, sin)
    km_rep = jnp.repeat(km.reshape(b, nb, d), N_HEADS, axis=1)
    sel = _moba_select(qft, km_rep)
    return _softmax_attn(qt, k, vt, sel, qk_rows=HEAD_DIM, name="moba_attn").reshape(n, d)


def _mla_proj_kernel(h_ref, g_ref, win_ref, gq_ref, gkv_ref, wuqt_ref, wk_ref, wvt_ref, e_ref,
                     cqt_ref, sqt_ref, ck_ref, sk_ref, qt_ref, k_ref, vt_ref, *, scale):
    xn = _rms(h_ref[0], g_ref[...]).astype(BF16)
    c = _dot(xn, win_ref[...])
    a0, a1, a2 = MLA_Q_RANK, MLA_Q_RANK + MLA_KV_RANK, MLA_Q_RANK + MLA_KV_RANK + MLA_ROPE
    cq = _rms(c[:, :a0], gq_ref[...]).astype(BF16)
    ckv = _rms(c[:, a0:a1], gkv_ref[...]).astype(BF16)
    kr = c[:, a1:a2] * ck_ref[...] + c[:, a2:a2 + MLA_ROPE] * sk_ref[...]
    cost, sint = jnp.tile(cqt_ref[...], (N_HEADS, 1)), jnp.tile(sqt_ref[...], (N_HEADS, 1))
    qt = _dot_nt(wuqt_ref[...], cq)
    qt = qt * cost + _swap_halves_t(qt, MLA_NOPE, MLA_ROPE, LANES) * sint
    k = _dot(ckv, wk_ref[...]) + _dot(kr.astype(BF16), e_ref[...])
    qt_ref[0, 0] = (qt * scale).astype(BF16)
    k_ref[0] = k.astype(BF16)
    vt_ref[0, 0] = _dot_nt(wvt_ref[...], ckv).astype(BF16)


def _mla_proj(h3, g, w_in_ext, gq, gkv, wuqt, wk, wvt, e, cqt, sqt, ck, sk):
    b, s, d = h3.shape
    t = ATTN_TILE
    hk, hv = N_HEADS * LANES, N_HEADS * MLA_V
    gq2, gkv2, g2 = gq.reshape(1, -1), gkv.reshape(1, -1), g.reshape(1, d)
    tab_t = pl.BlockSpec((LANES, t), lambda bi, i: (0, i))
    tab = pl.BlockSpec((t, MLA_ROPE), lambda bi, i: (i, 0))
    return pl.pallas_call(
        functools.partial(_mla_proj_kernel, scale=(MLA_NOPE + MLA_ROPE) ** -0.5 * LOG2E),
        grid=(b, s // t),
        in_specs=[_tok_spec(d), _full(g2), _full(w_in_ext), _full(gq2), _full(gkv2), _full(wuqt),
                  _full(wk), _full(wvt), _full(e), tab_t, tab_t, tab, tab],
        out_specs=[_tile_major_spec(hk), _tok_spec(hk), _tile_major_spec(hv)],
        out_shape=[_tile_major_shape(b, s, hk, BF16), jax.ShapeDtypeStruct((b, s, hk), BF16),
                   _tile_major_shape(b, s, hv, BF16)],
        compiler_params=_cparams("parallel", "parallel"),
        name="mla_proj",
    )(h3, g2, w_in_ext, gq2, gkv2, wuqt, wk, wvt, e, cqt, sqt, ck, sk)


def _mla_mixer(h, g, w_in, q_norm, w_uq, kv_norm, w_ukv, b, s):
    n, d = h.shape
    a1 = MLA_Q_RANK + MLA_KV_RANK
    pad = LANES - MLA_NOPE - MLA_ROPE
    swap = _swap_perm(MLA_ROPE, MLA_ROPE, 1)
    bf = lambda w: w.astype(BF16)
    in_cols = -(-(a1 + 2 * MLA_ROPE) // LANES) * LANES
    w_in_ext = bf(jnp.concatenate(
        [w_in, w_in[:, a1:][:, swap], jnp.zeros((d, in_cols - a1 - 2 * MLA_ROPE), w_in.dtype)], axis=1))
    wq = w_uq.reshape(MLA_Q_RANK, N_HEADS, MLA_NOPE + MLA_ROPE)
    zq = lambda w: jnp.zeros((MLA_Q_RANK, N_HEADS, w), w_uq.dtype)
    wuqt = bf(jnp.concatenate([wq, zq(pad)], axis=2).reshape(MLA_Q_RANK, -1).T)
    wkv = w_ukv.reshape(MLA_KV_RANK, N_HEADS, MLA_NOPE + MLA_V)
    wk = bf(jnp.concatenate([wkv[:, :, :MLA_NOPE], jnp.zeros((MLA_KV_RANK, N_HEADS, LANES - MLA_NOPE), w_ukv.dtype)],
                            axis=2).reshape(MLA_KV_RANK, -1))
    wvt = bf(wkv[:, :, MLA_NOPE:].reshape(MLA_KV_RANK, -1).T)
    eye = jnp.broadcast_to(jnp.eye(MLA_ROPE, dtype=BF16)[:, None, :], (MLA_ROPE, N_HEADS, MLA_ROPE))
    ze = lambda w: jnp.zeros((MLA_ROPE, N_HEADS, w), BF16)
    e = jnp.concatenate([ze(MLA_NOPE), eye, ze(pad)], axis=2).reshape(MLA_ROPE, -1)
    ck, sk = _rot_tables(s, MLA_ROPE, 0, MLA_ROPE)
    cq, sq = _rot_tables(s, MLA_ROPE, MLA_NOPE, LANES)
    qt, k, vt = _mla_proj(h.reshape(b, s, d), g, w_in_ext, q_norm, kv_norm, wuqt, wk, wvt, e,
                          cq.T, sq.T, ck, sk)
    return _softmax_attn(qt, k, vt, None, qk_rows=LANES, name="mla_attn").reshape(n, N_HEADS * MLA_V)


def kernel(x, norm_ffn1, ffn1_w_gate_up, ffn1_w_down, norm_mix, norm_ffn2, ffn2_w_gate_up, ffn2_w_down,
           sb_w_qkv, sb_w_o, moba_w_qkv, moba_w_o, mla_w_in, mla_q_norm, mla_w_uq, mla_kv_norm,
           mla_w_ukv, mla_w_o, final_norm):
    b, s, d = x.shape
    h = x.reshape(b * s, d)
    bf = lambda w: w.astype(BF16)
    for i in range(DEPTH):
        h = _ffn(h, None, norm_ffn1[i], bf(ffn1_w_gate_up[i]), bf(ffn1_w_down[i]), final_norm, final=False)
        kind, j = i % 3, i // 3
        if kind == 0:
            mix = (_sb_mixer(h, norm_mix[i], sb_w_qkv[j], b, s), bf(sb_w_o[j]))
        elif kind == 1:
            mix = (_moba_mixer(h, norm_mix[i], moba_w_qkv[j], b, s), bf(moba_w_o[j]))
        else:
            mix = (_mla_mixer(h, norm_mix[i], mla_w_in[j], mla_q_norm[j], mla_w_uq[j], mla_kv_norm[j],
                              mla_w_ukv[j], b, s), bf(mla_w_o[j]))
        h = _ffn(h, mix, norm_ffn2[i], bf(ffn2_w_gate_up[i]), bf(ffn2_w_down[i]), final_norm,
                 final=(i == DEPTH - 1))
    return h.reshape(b, s, d)
```

```python
import functools

import jax
import jax.numpy as jnp
from jax import lax
from jax.experimental import pallas as pl
from jax.experimental.pallas import tpu as pltpu

F32 = jnp.float32
BF16 = jnp.bfloat16

DEPTH = 4
N_HEADS = 16
HEAD_DIM = 64
ROPE_THETA = 500000.0
ROT_DIM = HEAD_DIM // 4
EPS = 1e-6
NEG = -1e30
LOG2E = 1.4426950408889634

MOBA_BLOCK = 256
MOBA_TOPK = 3

MLA_Q_RANK = 384
MLA_KV_RANK = 256
MLA_NOPE = 64
MLA_ROPE = 32
MLA_V = 64

LANES = 128
ATTN_TILE = 256
HEADS_PER_STEP = 8
SUM_ROWS = 16
VMEM_LIMIT = 48 * 1024 * 1024

_NT = (((1,), (1,)), ((), ()))


def _cparams(*sem):
    return pltpu.CompilerParams(dimension_semantics=sem, vmem_limit_bytes=VMEM_LIMIT)


def _rms(x, g):
    ms = jnp.mean(x * x, axis=-1, keepdims=True)
    return x * lax.rsqrt(ms + EPS) * g


def _dot(a, b):
    return jnp.dot(a, b, preferred_element_type=F32)


def _dot_nt(a, b):
    return lax.dot_general(a, b, _NT, preferred_element_type=F32)


def _ffn_kernel(*refs, final, mixed):
    if mixed:
        h_ref, a_ref, wo_ref, g_ref, wgu_ref, wd_ref, gf_ref, o_ref = refs
        h = h_ref[...] + _dot(a_ref[...], wo_ref[...])
    else:
        h_ref, g_ref, wgu_ref, wd_ref, gf_ref, o_ref = refs
        h = h_ref[...]
    dff = wd_ref.shape[0]
    gu = _dot(_rms(h, g_ref[...]).astype(BF16), wgu_ref[...])
    gate, up = gu[:, :dff], gu[:, dff:]
    act = (gate * jax.nn.sigmoid(gate) * up).astype(BF16)
    y = h + 0.5 * _dot(act, wd_ref[...])
    if final:
        y = _rms(y, gf_ref[...])
    o_ref[...] = y


def _ffn(h, mix, g, w_gate_up, w_down, g_final, *, final, tm=512):
    n, d = h.shape
    assert n % tm == 0
    resident = lambda a: pl.BlockSpec(a.shape, lambda i: (0, 0), pipeline_mode=pl.Buffered(1))
    rows = lambda w: pl.BlockSpec((tm, w), lambda i: (i, 0))
    vec = pl.BlockSpec((1, d), lambda i: (0, 0))
    mix_args = [] if mix is None else list(mix)
    mix_specs = [] if mix is None else [rows(mix[0].shape[1]), resident(mix[1])]
    return pl.pallas_call(
        functools.partial(_ffn_kernel, final=final, mixed=mix is not None),
        grid=(n // tm,),
        in_specs=[rows(d)] + mix_specs + [vec, resident(w_gate_up), resident(w_down), vec],
        out_specs=rows(d),
        out_shape=jax.ShapeDtypeStruct((n, d), F32),
        compiler_params=_cparams("parallel"),
        name="ffn_final" if final else ("ffn_mix" if mix is not None else "ffn"),
    )(h, *mix_args, g.reshape(1, d), w_gate_up, w_down, g_final.reshape(1, d))


def _full(a):
    return pl.BlockSpec(a.shape, lambda bi, i: (0,) * a.ndim)


def _tok_spec(width):
    return pl.BlockSpec((1, ATTN_TILE, width), lambda bi, i: (bi, i, 0))


def _tile_major_spec(rows):
    return pl.BlockSpec((1, 1, rows, ATTN_TILE), lambda bi, i: (bi, i, 0, 0))


def _tile_major_shape(b, s, rows, dtype):
    return jax.ShapeDtypeStruct((b, s // ATTN_TILE, rows, ATTN_TILE), dtype)


def _split_pair_t(x2):
    r = lax.broadcasted_iota(jnp.int32, x2.shape, 0)
    xf = x2.astype(F32)
    return (jnp.where(r < HEAD_DIM, xf, 0.0).astype(BF16),
            jnp.where(r < HEAD_DIM, 0.0, xf).astype(BF16))


def _rows128(ref, lead, p):
    return ref[lead + (slice(p * LANES, (p + 1) * LANES), slice(None))]


def _keys(k_ref, rows, p):
    return k_ref[0, rows, p * LANES:(p + 1) * LANES]


def _values_t(vt_ref, j, h):
    return vt_ref[0, j, h * HEAD_DIM:(h + 1) * HEAD_DIM, :]


def _store_heads_t(o_ref, outs):
    for p in range(len(outs) // 2):
        o2 = jnp.concatenate([outs[2 * p], outs[2 * p + 1]], axis=0)
        o_ref[0, :, p * LANES:(p + 1) * LANES] = o2.T.astype(o_ref.dtype)


def _sb_proj_kernel(h_ref, g_ref, wqt_ref, wk_ref, wvt_ref, qt_ref, k_ref, vt_ref, *, scale):
    xn = _rms(h_ref[0], g_ref[...]).astype(BF16)
    qt_ref[0, 0] = (_dot_nt(wqt_ref[...], xn) * scale).astype(BF16)
    k_ref[0] = _dot(xn, wk_ref[...]).astype(BF16)
    vt_ref[0, 0] = _dot_nt(wvt_ref[...], xn).astype(BF16)


def _sb_proj(h3, g, wqt, wk, wvt):
    b, s, d = h3.shape
    g2 = g.reshape(1, d)
    return pl.pallas_call(
        functools.partial(_sb_proj_kernel, scale=HEAD_DIM ** -0.5),
        grid=(b, s // ATTN_TILE),
        in_specs=[_tok_spec(d), _full(g2), _full(wqt), _full(wk), _full(wvt)],
        out_specs=[_tile_major_spec(d), _tok_spec(d), _tile_major_spec(d)],
        out_shape=[_tile_major_shape(b, s, d, BF16), jax.ShapeDtypeStruct((b, s, d), BF16),
                   _tile_major_shape(b, s, d, BF16)],
        compiler_params=_cparams("parallel", "parallel"),
        name="sb_proj",
    )(h3, g2, wqt, wk, wvt)


def _sb_attn_kernel(qt_ref, k_ref, vt_ref, o_ref, c_ref, acc_ref, *, t, g):
    i = pl.program_id(2)
    half = t // 2
    kk = lax.broadcasted_iota(jnp.int32, (half, half), 0)
    kj = lax.broadcasted_iota(jnp.int32, (half, half), 1)
    tri = jnp.where(kj >= kk, 1.0, 0.0).astype(BF16)
    tri2 = jnp.concatenate([tri, tri], axis=1)

    def inclusive_sums(hi, lo):
        upper = _dot(tri2, jnp.concatenate([hi[half:], lo[half:]], axis=0))
        lower = _dot(tri2, jnp.concatenate([hi[:half], lo[:half]], axis=0)) + upper[:1, :]
        return jnp.concatenate([lower, upper], axis=0)

    key = lax.broadcasted_iota(jnp.int32, (t, t), 0)
    qry = lax.broadcasted_iota(jnp.int32, (t, t), 1)
    past = key < qry
    qts = _query_operands(qt_ref, g, HEAD_DIM)

    def tile(j, diag):
        rows = pl.ds(pl.multiple_of(j * t, t), t)
        hs = range(g)
        zs = [_dot(_keys(k_ref, rows, h // 2), qts[h]) for h in hs]
        sps = [jnp.maximum(z, 0.0) + jnp.log(1.0 + jnp.exp2(jnp.abs(z) * -LOG2E)) for z in zs]
        if diag:
            sps = [jnp.where(past, sp, 0.0) for sp in sps]
        his = [sp.astype(BF16) for sp in sps]
        los = [(sps[h] - his[h].astype(F32)).astype(BF16) for h in hs]
        sums = [inclusive_sums(his[h], los[h]) for h in hs]
        ws = [jnp.exp(zs[h] - sums[h]) for h in hs]
        if diag:
            ws = [jnp.where(past, w, 0.0) for w in ws]
        for h in hs:
            c = c_ref[h]
            acc_ref[h] += jnp.exp(-c) * _dot(_values_t(vt_ref, j, h), ws[h].astype(BF16))
            c_ref[h] = c + sums[h][:1, :]

    c_ref[...] = jnp.zeros(c_ref.shape, F32)
    acc_ref[...] = jnp.zeros(acc_ref.shape, F32)
    tile(i, True)

    @pl.loop(0, i)
    def _(s):
        tile(i - 1 - s, False)

    _store_heads_t(o_ref, [acc_ref[h] for h in range(g)])


def _query_operands(qt_ref, g, qk_rows):
    if qk_rows == LANES:
        return [_rows128(qt_ref, (0, 0), h) for h in range(g)]
    qts = []
    for p in range(g // 2):
        qts.extend(_split_pair_t(_rows128(qt_ref, (0, 0), p)))
    return qts


def _attn_specs(s, t, g, qk_rows):
    wq, wv = g * qk_rows, g * HEAD_DIM
    nt = s // t
    return ([pl.BlockSpec((1, 1, wq, t), lambda bi, hi, i: (bi, i, hi, 0)),
             pl.BlockSpec((1, s, wq), lambda bi, hi, i: (bi, 0, hi)),
             pl.BlockSpec((1, nt, wv, t), lambda bi, hi, i: (bi, 0, hi, 0))],
            pl.BlockSpec((1, t, wv), lambda bi, hi, i: (bi, i, hi)))


def _sb_attn(qt, k, vt, *, g=HEADS_PER_STEP):
    b, s, d = k.shape
    t = ATTN_TILE
    in_specs, out_spec = _attn_specs(s, t, g, HEAD_DIM)
    return pl.pallas_call(
        functools.partial(_sb_attn_kernel, t=t, g=g),
        grid=(b, N_HEADS // g, s // t),
        in_specs=in_specs,
        out_specs=out_spec,
        out_shape=jax.ShapeDtypeStruct((b, s, d), BF16),
        scratch_shapes=[pltpu.VMEM((g, 1, t), F32), pltpu.VMEM((g, HEAD_DIM, t), F32)],
        compiler_params=_cparams("parallel", "parallel", "arbitrary"),
        name="sb_attn",
    )(qt, k, vt)


def _sb_mixer(h, g, w_qkv, b, s):
    n, d = h.shape
    wqt, wk, wvt = w_qkv[:, :d].T.astype(BF16), w_qkv[:, d:2 * d].astype(BF16), w_qkv[:, 2 * d:].T.astype(BF16)
    qt, k, vt = _sb_proj(h.reshape(b, s, d), g, wqt, wk, wvt)
    return _sb_attn(qt, k, vt).reshape(n, d)


def _softmax_attn_kernel(*refs, t, g, qk_rows, has_sel):
    if has_sel:
        qt_ref, k_ref, vt_ref, sel_ref, o_ref, m_ref, acc_ref = refs
    else:
        qt_ref, k_ref, vt_ref, o_ref, m_ref, acc_ref = refs
    i = pl.program_id(2)
    key = lax.broadcasted_iota(jnp.int32, (t, t), 0)
    qry = lax.broadcasted_iota(jnp.int32, (t, t), 1)
    causal = key <= qry
    qts = _query_operands(qt_ref, g, qk_rows)
    if qk_rows == HEAD_DIM:
        k_of = lambda rows, h: _keys(k_ref, rows, h // 2)
    else:
        k_of = lambda rows, h: _keys(k_ref, rows, h)
    ones = jnp.ones((SUM_ROWS, t), BF16)

    def tile(j, diag):
        rows = pl.ds(pl.multiple_of(j * t, t), t)
        hs = range(g)
        scs = [_dot(k_of(rows, h), qts[h]) for h in hs]
        if diag:
            scs = [jnp.where(causal, sc, NEG) for sc in scs]
        elif has_sel:
            sel = sel_ref[0, j]
            scs = [jnp.where(sel[h:h + 1, :] > 0.5, scs[h], NEG) for h in hs]
        ms = [m_ref[h] for h in hs]
        mns = [jnp.maximum(ms[h], jnp.max(scs[h], axis=0, keepdims=True)) for h in hs]
        alphas = [jnp.exp2(ms[h] - mns[h]) for h in hs]
        ps = [jnp.exp2(scs[h] - mns[h]) for h in hs]
        for h in hs:
            m_ref[h] = mns[h]
        for h in hs:
            vext = jnp.concatenate([_values_t(vt_ref, j, h), ones], axis=0)
            acc_ref[h] = alphas[h] * acc_ref[h] + _dot(vext, ps[h].astype(BF16))

    m_ref[...] = jnp.full(m_ref.shape, NEG, F32)
    acc_ref[...] = jnp.zeros(acc_ref.shape, F32)
    tile(i, True)

    @pl.loop(0, i)
    def _(j):
        tile(j, False)

    _store_heads_t(o_ref, [acc_ref[h, :HEAD_DIM] / acc_ref[h, HEAD_DIM:HEAD_DIM + 1] for h in range(g)])


def _softmax_attn(qt, k, vt, sel, *, qk_rows, name, g=HEADS_PER_STEP):
    b, s, _ = k.shape
    t = ATTN_TILE
    in_specs, out_spec = _attn_specs(s, t, g, qk_rows)
    args = [qt, k, vt]
    if sel is not None:
        in_specs.append(pl.BlockSpec((1, sel.shape[1], g, t), lambda bi, hi, i: (bi, 0, hi, i)))
        args.append(sel)
    return pl.pallas_call(
        functools.partial(_softmax_attn_kernel, t=t, g=g, qk_rows=qk_rows, has_sel=sel is not None),
        grid=(b, N_HEADS // g, s // t),
        in_specs=in_specs,
        out_specs=out_spec,
        out_shape=jax.ShapeDtypeStruct((b, s, N_HEADS * HEAD_DIM), BF16),
        scratch_shapes=[pltpu.VMEM((g, 1, t), F32), pltpu.VMEM((g, HEAD_DIM + SUM_ROWS, t), F32)],
        compiler_params=_cparams("parallel", "parallel", "arbitrary"),
        name=name,
    )(*args)


def _rot_tables(s, r, lead, width):
    inv = ROPE_THETA ** (-jnp.arange(0, r, 2, dtype=F32) / r)
    ang = jnp.arange(s).astype(F32)[:, None] * inv[None, :]
    cos, sin = jnp.cos(ang), jnp.sin(ang)
    one = lambda w: jnp.ones((s, w), F32)
    zero = lambda w: jnp.zeros((s, w), F32)
    rest = width - lead - r
    return (jnp.concatenate([one(lead), cos, cos, one(rest)], axis=-1),
            jnp.concatenate([zero(lead), -sin, sin, zero(rest)], axis=-1))


def _swap_perm(r, width, n):
    d = jnp.arange(width)
    p = jnp.where(d < r // 2, d + r // 2, jnp.where(d < r, d - r // 2, d))
    return (jnp.arange(n)[:, None] * width + p[None, :]).reshape(-1)


def _swap_halves_t(xt, lead, r, width):
    half = r // 2
    x4 = xt.reshape(xt.shape[0] // width, width // half, half, xt.shape[1])
    a = lead // half
    parts = [x4[:, :a], x4[:, a + 1:a + 2], x4[:, a:a + 1], x4[:, a + 2:]]
    return jnp.concatenate([p for p in parts if p.shape[1]], axis=1).reshape(xt.shape)


def _moba_proj_kernel(h_ref, g_ref, wqt_ref, wk_ref, wks_ref, wvt_ref,
                      cost_ref, sint_ref, cos_ref, sin_ref,
                      qt_ref, qft_ref, k_ref, vt_ref, km_ref, *, scale):
    reps = h_ref.shape[-1] // LANES
    xn = _rms(h_ref[0], g_ref[...]).astype(BF16)
    cost, sint = jnp.tile(cost_ref[...], (reps, 1)), jnp.tile(sint_ref[...], (reps, 1))
    cos, sin = jnp.tile(cos_ref[...], (1, reps)), jnp.tile(sin_ref[...], (1, reps))
    qt = _dot_nt(wqt_ref[...], xn)
    qt = qt * cost + _swap_halves_t(qt, 0, ROT_DIM, HEAD_DIM) * sint
    k = _dot(xn, wk_ref[...]) * cos + _dot(xn, wks_ref[...]) * sin
    qt_ref[0, 0] = (qt * scale).astype(BF16)
    qft_ref[0, 0] = qt
    k_ref[0] = k.astype(BF16)
    vt_ref[0, 0] = _dot_nt(wvt_ref[...], xn).astype(BF16)
    km_ref[0, 0] = jnp.mean(k, axis=0, keepdims=True)


def _moba_proj(h3, g, wqt, wk, wks, wvt, cost, sint, cos, sin):
    b, s, d = h3.shape
    t = ATTN_TILE
    g2 = g.reshape(1, d)
    tab_t = pl.BlockSpec((LANES, t), lambda bi, i: (0, i))
    tab = pl.BlockSpec((t, LANES), lambda bi, i: (i, 0))
    return pl.pallas_call(
        functools.partial(_moba_proj_kernel, scale=HEAD_DIM ** -0.5 * LOG2E),
        grid=(b, s // t),
        in_specs=[_tok_spec(d), _full(g2), _full(wqt), _full(wk), _full(wks), _full(wvt),
                  tab_t, tab_t, tab, tab],
        out_specs=[_tile_major_spec(d), _tile_major_spec(d), _tok_spec(d), _tile_major_spec(d),
                   pl.BlockSpec((1, 1, 1, d), lambda bi, i: (bi, i, 0, 0))],
        out_shape=[_tile_major_shape(b, s, d, BF16), _tile_major_shape(b, s, d, F32),
                   jax.ShapeDtypeStruct((b, s, d), BF16), _tile_major_shape(b, s, d, BF16),
                   jax.ShapeDtypeStruct((b, s // t, 1, d), F32)],
        compiler_params=_cparams("parallel", "parallel"),
        name="moba_proj",
    )(h3, g2, wqt, wk, wks, wvt, cost, sint, cos, sin)


def _moba_select_kernel(qt_ref, km_ref, sel_ref, *, nb):
    t = qt_ref.shape[-1]
    t0 = pl.program_id(1) * t
    qt = qt_ref[0, 0]
    km = km_ref[0]
    r = lax.broadcasted_iota(jnp.int32, km.shape, 0)
    c = lax.broadcasted_iota(jnp.int32, km.shape, 1)
    km = jnp.where((r % N_HEADS) == (c // HEAD_DIM), km, 0.0)
    qh = qt.astype(BF16)
    ql = (qt - qh.astype(F32)).astype(BF16)
    kh = km.astype(BF16)
    kl = (km - kh.astype(F32)).astype(BF16)
    gate = _dot(kh, qh) + _dot(kh, ql) + _dot(kl, qh)
    qblk = (t0 + lax.broadcasted_iota(jnp.int32, (N_HEADS, t), 1)) // MOBA_BLOCK
    gs = [jnp.where(n < qblk, gate[n * N_HEADS:(n + 1) * N_HEADS, :], NEG) for n in range(nb)]
    for n in range(nb):
        rank = jnp.zeros((N_HEADS, t), F32)
        for m in range(nb):
            if m == n:
                continue
            beats = (gs[m] >= gs[n]) if m < n else (gs[m] > gs[n])
            rank = rank + jnp.where(beats, 1.0, 0.0)
        keep = jnp.where(rank < MOBA_TOPK - 0.5, 1.0, 0.0)
        sel_ref[0, n] = jnp.where(n < qblk, keep, 0.0)


def _moba_select(qft, km_rep):
    b, nt, d, t = qft.shape
    s = nt * t
    nb = s // MOBA_BLOCK
    return pl.pallas_call(
        functools.partial(_moba_select_kernel, nb=nb),
        grid=(b, nt),
        in_specs=[_tile_major_spec(d), pl.BlockSpec((1, nb * N_HEADS, d), lambda bi, i: (bi, 0, 0))],
        out_specs=pl.BlockSpec((1, nb, N_HEADS, t), lambda bi, i: (bi, 0, 0, i)),
        out_shape=jax.ShapeDtypeStruct((b, nb, N_HEADS, s), F32),
        compiler_params=_cparams("parallel", "parallel"),
        name="moba_select",
    )(qft, km_rep)


def _moba_mixer(h, g, w_qkv, b, s):
    n, d = h.shape
    assert s % MOBA_BLOCK == 0 and MOBA_BLOCK == ATTN_TILE
    nb = s // MOBA_BLOCK
    perm = _swap_perm(ROT_DIM, HEAD_DIM, N_HEADS)
    wq, wk, wv = w_qkv[:, :d], w_qkv[:, d:2 * d], w_qkv[:, 2 * d:]
    bf = lambda w: w.astype(BF16)
    cos, sin = (jnp.tile(x, (1, LANES // HEAD_DIM)) for x in _rot_tables(s, ROT_DIM, 0, HEAD_DIM))
    qt, qft, k, vt, km = _moba_proj(h.reshape(b, s, d), g, bf(wq.T), bf(wk), bf(wk[:, perm]),
                                    bf(wv.T), cos.T, sin.T, cos, sin)
    km_rep = jnp.repeat(km.reshape(b, nb, d), N_HEADS, axis=1)
    sel = _moba_select(qft, km_rep)
    return _softmax_attn(qt, k, vt, sel, qk_rows=HEAD_DIM, name="moba_attn").reshape(n, d)


def _mla_proj_kernel(h_ref, g_ref, win_ref, gq_ref, gkv_ref, wuqt_ref, wk_ref, wvt_ref, e_ref,
                     cqt_ref, sqt_ref, ck_ref, sk_ref, qt_ref, k_ref, vt_ref, *, scale):
    xn = _rms(h_ref[0], g_ref[...]).astype(BF16)
    c = _dot(xn, win_ref[...])
    a0, a1, a2 = MLA_Q_RANK, MLA_Q_RANK + MLA_KV_RANK, MLA_Q_RANK + MLA_KV_RANK + MLA_ROPE
    cq = _rms(c[:, :a0], gq_ref[...]).astype(BF16)
    ckv = _rms(c[:, a0:a1], gkv_ref[...]).astype(BF16)
    kr = c[:, a1:a2] * ck_ref[...] + c[:, a2:a2 + MLA_ROPE] * sk_ref[...]
    cost, sint = jnp.tile(cqt_ref[...], (N_HEADS, 1)), jnp.tile(sqt_ref[...], (N_HEADS, 1))
    qt = _dot_nt(wuqt_ref[...], cq)
    qt = qt * cost + _swap_halves_t(qt, MLA_NOPE, MLA_ROPE, LANES) * sint
    k = _dot(ckv, wk_ref[...]) + _dot(kr.astype(BF16), e_ref[...])
    qt_ref[0, 0] = (qt * scale).astype(BF16)
    k_ref[0] = k.astype(BF16)
    vt_ref[0, 0] = _dot_nt(wvt_ref[...], ckv).astype(BF16)


def _mla_proj(h3, g, w_in_ext, gq, gkv, wuqt, wk, wvt, e, cqt, sqt, ck, sk):
    b, s, d = h3.shape
    t = ATTN_TILE
    hk, hv = N_HEADS * LANES, N_HEADS * MLA_V
    gq2, gkv2, g2 = gq.reshape(1, -1), gkv.reshape(1, -1), g.reshape(1, d)
    tab_t = pl.BlockSpec((LANES, t), lambda bi, i: (0, i))
    tab = pl.BlockSpec((t, MLA_ROPE), lambda bi, i: (i, 0))
    return pl.pallas_call(
        functools.partial(_mla_proj_kernel, scale=(MLA_NOPE + MLA_ROPE) ** -0.5 * LOG2E),
        grid=(b, s // t),
        in_specs=[_tok_spec(d), _full(g2), _full(w_in_ext), _full(gq2), _full(gkv2), _full(wuqt),
                  _full(wk), _full(wvt), _full(e), tab_t, tab_t, tab, tab],
        out_specs=[_tile_major_spec(hk), _tok_spec(hk), _tile_major_spec(hv)],
        out_shape=[_tile_major_shape(b, s, hk, BF16), jax.ShapeDtypeStruct((b, s, hk), BF16),
                   _tile_major_shape(b, s, hv, BF16)],
        compiler_params=_cparams("parallel", "parallel"),
        name="mla_proj",
    )(h3, g2, w_in_ext, gq2, gkv2, wuqt, wk, wvt, e, cqt, sqt, ck, sk)


def _mla_mixer(h, g, w_in, q_norm, w_uq, kv_norm, w_ukv, b, s):
    n, d = h.shape
    a1 = MLA_Q_RANK + MLA_KV_RANK
    pad = LANES - MLA_NOPE - MLA_ROPE
    swap = _swap_perm(MLA_ROPE, MLA_ROPE, 1)
    bf = lambda w: w.astype(BF16)
    in_cols = -(-(a1 + 2 * MLA_ROPE) // LANES) * LANES
    w_in_ext = bf(jnp.concatenate(
        [w_in, w_in[:, a1:][:, swap], jnp.zeros((d, in_cols - a1 - 2 * MLA_ROPE), w_in.dtype)], axis=1))
    wq = w_uq.reshape(MLA_Q_RANK, N_HEADS, MLA_NOPE + MLA_ROPE)
    zq = lambda w: jnp.zeros((MLA_Q_RANK, N_HEADS, w), w_uq.dtype)
    wuqt = bf(jnp.concatenate([wq, zq(pad)], axis=2).reshape(MLA_Q_RANK, -1).T)
    wkv = w_ukv.reshape(MLA_KV_RANK, N_HEADS, MLA_NOPE + MLA_V)
    wk = bf(jnp.concatenate([wkv[:, :, :MLA_NOPE], jnp.zeros((MLA_KV_RANK, N_HEADS, LANES - MLA_NOPE), w_ukv.dtype)],
                            axis=2).reshape(MLA_KV_RANK, -1))
    wvt = bf(wkv[:, :, MLA_NOPE:].reshape(MLA_KV_RANK, -1).T)
    eye = jnp.broadcast_to(jnp.eye(MLA_ROPE, dtype=BF16)[:, None, :], (MLA_ROPE, N_HEADS, MLA_ROPE))
    ze = lambda w: jnp.zeros((MLA_ROPE, N_HEADS, w), BF16)
    e = jnp.concatenate([ze(MLA_NOPE), eye, ze(pad)], axis=2).reshape(MLA_ROPE, -1)
    ck, sk = _rot_tables(s, MLA_ROPE, 0, MLA_ROPE)
    cq, sq = _rot_tables(s, MLA_ROPE, MLA_NOPE, LANES)
    qt, k, vt = _mla_proj(h.reshape(b, s, d), g, w_in_ext, q_norm, kv_norm, wuqt, wk, wvt, e,
                          cq.T, sq.T, ck, sk)
    return _softmax_attn(qt, k, vt, None, qk_rows=LANES, name="mla_attn").reshape(n, N_HEADS * MLA_V)


def kernel(x, norm_ffn1, ffn1_w_gate_up, ffn1_w_down, norm_mix, norm_ffn2, ffn2_w_gate_up, ffn2_w_down,
           sb_w_qkv, sb_w_o, moba_w_qkv, moba_w_o, mla_w_in, mla_q_norm, mla_w_uq, mla_kv_norm,
           mla_w_ukv, mla_w_o, final_norm):
    b, s, d = x.shape
    h = x.reshape(b * s, d)
    bf = lambda w: w.astype(BF16)
    for i in range(DEPTH):
        h = _ffn(h, None, norm_ffn1[i], bf(ffn1_w_gate_up[i]), bf(ffn1_w_down[i]), final_norm, final=False)
        kind, j = i % 3, i // 3
        if kind == 0:
            mix = (_sb_mixer(h, norm_mix[i], sb_w_qkv[j], b, s), bf(sb_w_o[j]))
        elif kind == 1:
            mix = (_moba_mixer(h, norm_mix[i], moba_w_qkv[j], b, s), bf(moba_w_o[j]))
        else:
            mix = (_mla_mixer(h, norm_mix[i], mla_w_in[j], mla_q_norm[j], mla_w_uq[j], mla_kv_norm[j],
                              mla_w_ukv[j], b, s), bf(mla_w_o[j]))
        h = _ffn(h, mix, norm_ffn2[i], bf(ffn2_w_gate_up[i]), bf(ffn2_w_down[i]), final_norm,
                 final=(i == DEPTH - 1))
    return h.reshape(b, s, d)
```

```python
import functools

import jax
import jax.numpy as jnp
from jax import lax
from jax.experimental import pallas as pl
from jax.experimental.pallas import tpu as pltpu

F32 = jnp.float32
BF16 = jnp.bfloat16

DEPTH = 4
N_HEADS = 16
HEAD_DIM = 64
ROPE_THETA = 500000.0
ROT_DIM = HEAD_DIM // 4
EPS = 1e-6
NEG = -1e30
LOG2E = 1.4426950408889634

MOBA_BLOCK = 256
MOBA_TOPK = 3

MLA_Q_RANK = 384
MLA_KV_RANK = 256
MLA_NOPE = 64
MLA_ROPE = 32
MLA_V = 64

LANES = 128
ATTN_TILE = 256
HEADS_PER_STEP = 8
SUM_ROWS = 16
VMEM_LIMIT = 48 * 1024 * 1024

_NT = (((1,), (1,)), ((), ()))


def _cparams(*sem):
    return pltpu.CompilerParams(dimension_semantics=sem, vmem_limit_bytes=VMEM_LIMIT)


def _rms(x, g):
    ms = jnp.mean(x * x, axis=-1, keepdims=True)
    return x * lax.rsqrt(ms + EPS) * g


def _dot(a, b):
    return jnp.dot(a, b, preferred_element_type=F32)


def _dot_nt(a, b):
    return lax.dot_general(a, b, _NT, preferred_element_type=F32)


def _ffn_kernel(*refs, final, mixed):
    if mixed:
        h_ref, a_ref, wo_ref, g_ref, wgu_ref, wd_ref, gf_ref, o_ref = refs
        h = h_ref[...] + _dot(a_ref[...], wo_ref[...])
    else:
        h_ref, g_ref, wgu_ref, wd_ref, gf_ref, o_ref = refs
        h = h_ref[...]
    dff = wd_ref.shape[0]
    gu = _dot(_rms(h, g_ref[...]).astype(BF16), wgu_ref[...])
    gate, up = gu[:, :dff], gu[:, dff:]
    act = (gate * jax.nn.sigmoid(gate) * up).astype(BF16)
    y = h + 0.5 * _dot(act, wd_ref[...])
    if final:
        y = _rms(y, gf_ref[...])
    o_ref[...] = y


def _ffn(h, mix, g, w_gate_up, w_down, g_final, *, final, tm=512):
    n, d = h.shape
    assert n % tm == 0
    resident = lambda a: pl.BlockSpec(a.shape, lambda i: (0, 0), pipeline_mode=pl.Buffered(1))
    rows = lambda w: pl.BlockSpec((tm, w), lambda i: (i, 0))
    vec = pl.BlockSpec((1, d), lambda i: (0, 0))
    mix_args = [] if mix is None else list(mix)
    mix_specs = [] if mix is None else [rows(mix[0].shape[1]), resident(mix[1])]
    return pl.pallas_call(
        functools.partial(_ffn_kernel, final=final, mixed=mix is not None),
        grid=(n // tm,),
        in_specs=[rows(d)] + mix_specs + [vec, resident(w_gate_up), resident(w_down), vec],
        out_specs=rows(d),
        out_shape=jax.ShapeDtypeStruct((n, d), F32),
        compiler_params=_cparams("parallel"),
        name="ffn_final" if final else ("ffn_mix" if mix is not None else "ffn"),
    )(h, *mix_args, g.reshape(1, d), w_gate_up, w_down, g_final.reshape(1, d))


def _full(a):
    return pl.BlockSpec(a.shape, lambda bi, i: (0,) * a.ndim)


def _tok_spec(width):
    return pl.BlockSpec((1, ATTN_TILE, width), lambda bi, i: (bi, i, 0))


def _tile_major_spec(rows):
    return pl.BlockSpec((1, 1, rows, ATTN_TILE), lambda bi, i: (bi, i, 0, 0))


def _tile_major_shape(b, s, rows, dtype):
    return jax.ShapeDtypeStruct((b, s // ATTN_TILE, rows, ATTN_TILE), dtype)


def _split_pair_t(x2):
    r = lax.broadcasted_iota(jnp.int32, x2.shape, 0)
    xf = x2.astype(F32)
    return (jnp.where(r < HEAD_DIM, xf, 0.0).astype(BF16),
            jnp.where(r < HEAD_DIM, 0.0, xf).astype(BF16))


def _rows128(ref, lead, p):
    return ref[lead + (slice(p * LANES, (p + 1) * LANES), slice(None))]


def _keys(k_ref, rows, p):
    return k_ref[0, rows, p * LANES:(p + 1) * LANES]


def _values_t(vt_ref, j, h):
    return vt_ref[0, j, h * HEAD_DIM:(h + 1) * HEAD_DIM, :]


def _store_heads_t(o_ref, outs):
    for p in range(len(outs) // 2):
        o2 = jnp.concatenate([outs[2 * p], outs[2 * p + 1]], axis=0)
        o_ref[0, :, p * LANES:(p + 1) * LANES] = o2.T.astype(o_ref.dtype)


def _sb_proj_kernel(h_ref, g_ref, wqt_ref, wk_ref, wvt_ref, qt_ref, k_ref, vt_ref, *, scale):
    xn = _rms(h_ref[0], g_ref[...]).astype(BF16)
    qt_ref[0, 0] = (_dot_nt(wqt_ref[...], xn) * scale).astype(BF16)
    k_ref[0] = _dot(xn, wk_ref[...]).astype(BF16)
    vt_ref[0, 0] = _dot_nt(wvt_ref[...], xn).astype(BF16)


def _sb_proj(h3, g, wqt, wk, wvt):
    b, s, d = h3.shape
    g2 = g.reshape(1, d)
    return pl.pallas_call(
        functools.partial(_sb_proj_kernel, scale=HEAD_DIM ** -0.5),
        grid=(b, s // ATTN_TILE),
        in_specs=[_tok_spec(d), _full(g2), _full(wqt), _full(wk), _full(wvt)],
        out_specs=[_tile_major_spec(d), _tok_spec(d), _tile_major_spec(d)],
        out_shape=[_tile_major_shape(b, s, d, BF16), jax.ShapeDtypeStruct((b, s, d), BF16),
                   _tile_major_shape(b, s, d, BF16)],
        compiler_params=_cparams("parallel", "parallel"),
        name="sb_proj",
    )(h3, g2, wqt, wk, wvt)


def _sb_attn_kernel(qt_ref, k_ref, vt_ref, o_ref, c_ref, acc_ref, *, t, g):
    i = pl.program_id(2)
    half = t // 2
    kk = lax.broadcasted_iota(jnp.int32, (half, half), 0)
    kj = lax.broadcasted_iota(jnp.int32, (half, half), 1)
    tri = jnp.where(kj >= kk, 1.0, 0.0).astype(BF16)
    tri2 = jnp.concatenate([tri, tri], axis=1)

    def inclusive_sums(hi, lo):
        upper = _dot(tri2, jnp.concatenate([hi[half:], lo[half:]], axis=0))
        lower = _dot(tri2, jnp.concatenate([hi[:half], lo[:half]], axis=0)) + upper[:1, :]
        return jnp.concatenate([lower, upper], axis=0)

    key = lax.broadcasted_iota(jnp.int32, (t, t), 0)
    qry = lax.broadcasted_iota(jnp.int32, (t, t), 1)
    past = key < qry
    qts = _query_operands(qt_ref, g, HEAD_DIM)

    def tile(j, diag):
        rows = pl.ds(pl.multiple_of(j * t, t), t)
        hs = range(g)
        zs = [_dot(_keys(k_ref, rows, h // 2), qts[h]) for h in hs]
        sps = [jnp.maximum(z, 0.0) + jnp.log(1.0 + jnp.exp2(jnp.abs(z) * -LOG2E)) for z in zs]
        if diag:
            sps = [jnp.where(past, sp, 0.0) for sp in sps]
        his = [sp.astype(BF16) for sp in sps]
        los = [(sps[h] - his[h].astype(F32)).astype(BF16) for h in hs]
        sums = [inclusive_sums(his[h], los[h]) for h in hs]
        ws = [jnp.exp(zs[h] - sums[h]) for h in hs]
        if diag:
            ws = [jnp.where(past, w, 0.0) for w in ws]
        for h in hs:
            c = c_ref[h]
            acc_ref[h] += jnp.exp(-c) * _dot(_values_t(vt_ref, j, h), ws[h].astype(BF16))
            c_ref[h] = c + sums[h][:1, :]

    c_ref[...] = jnp.zeros(c_ref.shape, F32)
    acc_ref[...] = jnp.zeros(acc_ref.shape, F32)
    tile(i, True)

    @pl.loop(0, i)
    def _(s):
        tile(i - 1 - s, False)

    _store_heads_t(o_ref, [acc_ref[h] for h in range(g)])


def _query_operands(qt_ref, g, qk_rows):
    if qk_rows == LANES:
        return [_rows128(qt_ref, (0, 0), h) for h in range(g)]
    qts = []
    for p in range(g // 2):
        qts.extend(_split_pair_t(_rows128(qt_ref, (0, 0), p)))
    return qts


def _attn_specs(s, t, g, qk_rows):
    wq, wv = g * qk_rows, g * HEAD_DIM
    nt = s // t
    return ([pl.BlockSpec((1, 1, wq, t), lambda bi, hi, i: (bi, i, hi, 0)),
             pl.BlockSpec((1, s, wq), lambda bi, hi, i: (bi, 0, hi)),
             pl.BlockSpec((1, nt, wv, t), lambda bi, hi, i: (bi, 0, hi, 0))],
            pl.BlockSpec((1, t, wv), lambda bi, hi, i: (bi, i, hi)))


def _sb_attn(qt, k, vt, *, g=HEADS_PER_STEP):
    b, s, d = k.shape
    t = ATTN_TILE
    in_specs, out_spec = _attn_specs(s, t, g, HEAD_DIM)
    return pl.pallas_call(
        functools.partial(_sb_attn_kernel, t=t, g=g),
        grid=(b, N_HEADS // g, s // t),
        in_specs=in_specs,
        out_specs=out_spec,
        out_shape=jax.ShapeDtypeStruct((b, s, d), BF16),
        scratch_shapes=[pltpu.VMEM((g, 1, t), F32), pltpu.VMEM((g, HEAD_DIM, t), F32)],
        compiler_params=_cparams("parallel", "parallel", "arbitrary"),
        name="sb_attn",
    )(qt, k, vt)


def _sb_mixer(h, g, w_qkv, b, s):
    n, d = h.shape
    wqt, wk, wvt = w_qkv[:, :d].T.astype(BF16), w_qkv[:, d:2 * d].astype(BF16), w_qkv[:, 2 * d:].T.astype(BF16)
    qt, k, vt = _sb_proj(h.reshape(b, s, d), g, wqt, wk, wvt)
    return _sb_attn(qt, k, vt).reshape(n, d)


def _softmax_attn_kernel(*refs, t, g, qk_rows, has_sel):
    if has_sel:
        qt_ref, k_ref, vt_ref, sel_ref, o_ref, m_ref, acc_ref = refs
    else:
        qt_ref, k_ref, vt_ref, o_ref, m_ref, acc_ref = refs
    i = pl.program_id(2)
    half = t // 2
    qts = _query_operands(qt_ref, g, qk_rows)
    if qk_rows == HEAD_DIM:
        k_of = lambda rows, h: _keys(k_ref, rows, h // 2)
    else:
        k_of = lambda rows, h: _keys(k_ref, rows, h)
    ones = jnp.ones((SUM_ROWS, t), BF16)

    def part(j, k0, nk, q0, nq, diag):
        rows = pl.ds(pl.multiple_of(j * t + k0, half), nk)
        qs = slice(q0, q0 + nq)
        hs = range(g)
        scs = [_dot(k_of(rows, h), qts[h][:, qs]) for h in hs]
        if diag:
            key = lax.broadcasted_iota(jnp.int32, (nk, nq), 0) + k0
            qry = lax.broadcasted_iota(jnp.int32, (nk, nq), 1) + q0
            scs = [jnp.where(key <= qry, sc, NEG) for sc in scs]
        elif has_sel:
            sel = sel_ref[0, j]
            scs = [jnp.where(sel[h:h + 1, qs] > 0.5, scs[h], NEG) for h in hs]
        ms = [m_ref[h, :, qs] for h in hs]
        mns = [jnp.maximum(ms[h], jnp.max(scs[h], axis=0, keepdims=True)) for h in hs]
        alphas = [jnp.exp2(ms[h] - mns[h]) for h in hs]
        ps = [jnp.exp2(scs[h] - mns[h]) for h in hs]
        for h in hs:
            m_ref[h, :, qs] = mns[h]
        for h in hs:
            vext = jnp.concatenate([_values_t(vt_ref, j, h)[:, k0:k0 + nk], ones[:, :nk]], axis=0)
            acc_ref[h, :, qs] = alphas[h] * acc_ref[h, :, qs] + _dot(vext, ps[h].astype(BF16))

    m_ref[...] = jnp.full(m_ref.shape, NEG, F32)
    acc_ref[...] = jnp.zeros(acc_ref.shape, F32)
    part(i, 0, half, 0, t, True)
    part(i, half, half, half, half, True)

    @pl.loop(0, i)
    def _(j):
        part(j, 0, t, 0, t, False)

    _store_heads_t(o_ref, [acc_ref[h, :HEAD_DIM] / acc_ref[h, HEAD_DIM:HEAD_DIM + 1] for h in range(g)])


def _softmax_attn(qt, k, vt, sel, *, qk_rows, name, g=HEADS_PER_STEP):
    b, s, _ = k.shape
    t = ATTN_TILE
    in_specs, out_spec = _attn_specs(s, t, g, qk_rows)
    args = [qt, k, vt]
    if sel is not None:
        in_specs.append(pl.BlockSpec((1, sel.shape[1], g, t), lambda bi, hi, i: (bi, 0, hi, i)))
        args.append(sel)
    return pl.pallas_call(
        functools.partial(_softmax_attn_kernel, t=t, g=g, qk_rows=qk_rows, has_sel=sel is not None),
        grid=(b, N_HEADS // g, s // t),
        in_specs=in_specs,
        out_specs=out_spec,
        out_shape=jax.ShapeDtypeStruct((b, s, N_HEADS * HEAD_DIM), BF16),
        scratch_shapes=[pltpu.VMEM((g, 1, t), F32), pltpu.VMEM((g, HEAD_DIM + SUM_ROWS, t), F32)],
        compiler_params=_cparams("parallel", "parallel", "arbitrary"),
        name=name,
    )(*args)


def _rot_tables(s, r, lead, width):
    inv = ROPE_THETA ** (-jnp.arange(0, r, 2, dtype=F32) / r)
    ang = jnp.arange(s).astype(F32)[:, None] * inv[None, :]
    cos, sin = jnp.cos(ang), jnp.sin(ang)
    one = lambda w: jnp.ones((s, w), F32)
    zero = lambda w: jnp.zeros((s, w), F32)
    rest = width - lead - r
    return (jnp.concatenate([one(lead), cos, cos, one(rest)], axis=-1),
            jnp.concatenate([zero(lead), -sin, sin, zero(rest)], axis=-1))


def _swap_perm(r, width, n):
    d = jnp.arange(width)
    p = jnp.where(d < r // 2, d + r // 2, jnp.where(d < r, d - r // 2, d))
    return (jnp.arange(n)[:, None] * width + p[None, :]).reshape(-1)


def _swap_halves_t(xt, lead, r, width):
    half = r // 2
    x4 = xt.reshape(xt.shape[0] // width, width // half, half, xt.shape[1])
    a = lead // half
    parts = [x4[:, :a], x4[:, a + 1:a + 2], x4[:, a:a + 1], x4[:, a + 2:]]
    return jnp.concatenate([p for p in parts if p.shape[1]], axis=1).reshape(xt.shape)


def _moba_proj_kernel(h_ref, g_ref, wqt_ref, wk_ref, wks_ref, wvt_ref,
                      cost_ref, sint_ref, cos_ref, sin_ref,
                      qt_ref, qft_ref, k_ref, vt_ref, km_ref, *, scale):
    reps = h_ref.shape[-1] // LANES
    xn = _rms(h_ref[0], g_ref[...]).astype(BF16)
    cost, sint = jnp.tile(cost_ref[...], (reps, 1)), jnp.tile(sint_ref[...], (reps, 1))
    cos, sin = jnp.tile(cos_ref[...], (1, reps)), jnp.tile(sin_ref[...], (1, reps))
    qt = _dot_nt(wqt_ref[...], xn)
    qt = qt * cost + _swap_halves_t(qt, 0, ROT_DIM, HEAD_DIM) * sint
    k = _dot(xn, wk_ref[...]) * cos + _dot(xn, wks_ref[...]) * sin
    qt_ref[0, 0] = (qt * scale).astype(BF16)
    qft_ref[0, 0] = qt
    k_ref[0] = k.astype(BF16)
    vt_ref[0, 0] = _dot_nt(wvt_ref[...], xn).astype(BF16)
    km_ref[0, 0] = jnp.mean(k, axis=0, keepdims=True)


def _moba_proj(h3, g, wqt, wk, wks, wvt, cost, sint, cos, sin):
    b, s, d = h3.shape
    t = ATTN_TILE
    g2 = g.reshape(1, d)
    tab_t = pl.BlockSpec((LANES, t), lambda bi, i: (0, i))
    tab = pl.BlockSpec((t, LANES), lambda bi, i: (i, 0))
    return pl.pallas_call(
        functools.partial(_moba_proj_kernel, scale=HEAD_DIM ** -0.5 * LOG2E),
        grid=(b, s // t),
        in_specs=[_tok_spec(d), _full(g2), _full(wqt), _full(wk), _full(wks), _full(wvt),
                  tab_t, tab_t, tab, tab],
        out_specs=[_tile_major_spec(d), _tile_major_spec(d), _tok_spec(d), _tile_major_spec(d),
                   pl.BlockSpec((1, 1, 1, d), lambda bi, i: (bi, i, 0, 0))],
        out_shape=[_tile_major_shape(b, s, d, BF16), _tile_major_shape(b, s, d, F32),
                   jax.ShapeDtypeStruct((b, s, d), BF16), _tile_major_shape(b, s, d, BF16),
                   jax.ShapeDtypeStruct((b, s // t, 1, d), F32)],
        compiler_params=_cparams("parallel", "parallel"),
        name="moba_proj",
    )(h3, g2, wqt, wk, wks, wvt, cost, sint, cos, sin)


def _moba_select_kernel(qt_ref, km_ref, sel_ref, *, nb):
    t = qt_ref.shape[-1]
    t0 = pl.program_id(1) * t
    qt = qt_ref[0, 0]
    km = km_ref[0]
    r = lax.broadcasted_iota(jnp.int32, km.shape, 0)
    c = lax.broadcasted_iota(jnp.int32, km.shape, 1)
    km = jnp.where((r % N_HEADS) == (c // HEAD_DIM), km, 0.0)
    qh = qt.astype(BF16)
    ql = (qt - qh.astype(F32)).astype(BF16)
    kh = km.astype(BF16)
    kl = (km - kh.astype(F32)).astype(BF16)
    gate = _dot(kh, qh) + _dot(kh, ql) + _dot(kl, qh)
    qblk = (t0 + lax.broadcasted_iota(jnp.int32, (N_HEADS, t), 1)) // MOBA_BLOCK
    gs = [jnp.where(n < qblk, gate[n * N_HEADS:(n + 1) * N_HEADS, :], NEG) for n in range(nb)]
    for n in range(nb):
        rank = jnp.zeros((N_HEADS, t), F32)
        for m in range(nb):
            if m == n:
                continue
            beats = (gs[m] >= gs[n]) if m < n else (gs[m] > gs[n])
            rank = rank + jnp.where(beats, 1.0, 0.0)
        keep = jnp.where(rank < MOBA_TOPK - 0.5, 1.0, 0.0)
        sel_ref[0, n] = jnp.where(n < qblk, keep, 0.0)


def _moba_select(qft, km_rep):
    b, nt, d, t = qft.shape
    s = nt * t
    nb = s // MOBA_BLOCK
    return pl.pallas_call(
        functools.partial(_moba_select_kernel, nb=nb),
        grid=(b, nt),
        in_specs=[_tile_major_spec(d), pl.BlockSpec((1, nb * N_HEADS, d), lambda bi, i: (bi, 0, 0))],
        out_specs=pl.BlockSpec((1, nb, N_HEADS, t), lambda bi, i: (bi, 0, 0, i)),
        out_shape=jax.ShapeDtypeStruct((b, nb, N_HEADS, s), F32),
        compiler_params=_cparams("parallel", "parallel"),
        name="moba_select",
    )(qft, km_rep)


def _moba_mixer(h, g, w_qkv, b, s):
    n, d = h.shape
    assert s % MOBA_BLOCK == 0 and MOBA_BLOCK == ATTN_TILE
    nb = s // MOBA_BLOCK
    perm = _swap_perm(ROT_DIM, HEAD_DIM, N_HEADS)
    wq, wk, wv = w_qkv[:, :d], w_qkv[:, d:2 * d], w_qkv[:, 2 * d:]
    bf = lambda w: w.astype(BF16)
    cos, sin = (jnp.tile(x, (1, LANES // HEAD_DIM)) for x in _rot_tables(s, ROT_DIM, 0, HEAD_DIM))
    qt, qft, k, vt, km = _moba_proj(h.reshape(b, s, d), g, bf(wq.T), bf(wk), bf(wk[:, perm]),
                                    bf(wv.T), cos.T, sin.T, cos, sin)
    km_rep = jnp.repeat(km.reshape(b, nb, d), N_HEADS, axis=1)
    sel = _moba_select(qft, km_rep)
    return _softmax_attn(qt, k, vt, sel, qk_rows=HEAD_DIM, name="moba_attn").reshape(n, d)


def _mla_proj_kernel(h_ref, g_ref, win_ref, gq_ref, gkv_ref, wuqt_ref, wk_ref, wvt_ref, e_ref,
                     cqt_ref, sqt_ref, ck_ref, sk_ref, qt_ref, k_ref, vt_ref, *, scale):
    xn = _rms(h_ref[0], g_ref[...]).astype(BF16)
    c = _dot(xn, win_ref[...])
    a0, a1, a2 = MLA_Q_RANK, MLA_Q_RANK + MLA_KV_RANK, MLA_Q_RANK + MLA_KV_RANK + MLA_ROPE
    cq = _rms(c[:, :a0], gq_ref[...]).astype(BF16)
    ckv = _rms(c[:, a0:a1], gkv_ref[...]).astype(BF16)
    kr = c[:, a1:a2] * ck_ref[...] + c[:, a2:a2 + MLA_ROPE] * sk_ref[...]
    cost, sint = jnp.tile(cqt_ref[...], (N_HEADS, 1)), jnp.tile(sqt_ref[...], (N_HEADS, 1))
    qt = _dot_nt(wuqt_ref[...], cq)
    qt = qt * cost + _swap_halves_t(qt, MLA_NOPE, MLA_ROPE, LANES) * sint
    k = _dot(ckv, wk_ref[...]) + _dot(kr.astype(BF16), e_ref[...])
    qt_ref[0, 0] = (qt * scale).astype(BF16)
    k_ref[0] = k.astype(BF16)
    vt_ref[0, 0] = _dot_nt(wvt_ref[...], ckv).astype(BF16)


def _mla_proj(h3, g, w_in_ext, gq, gkv, wuqt, wk, wvt, e, cqt, sqt, ck, sk):
    b, s, d = h3.shape
    t = ATTN_TILE
    hk, hv = N_HEADS * LANES, N_HEADS * MLA_V
    gq2, gkv2, g2 = gq.reshape(1, -1), gkv.reshape(1, -1), g.reshape(1, d)
    tab_t = pl.BlockSpec((LANES, t), lambda bi, i: (0, i))
    tab = pl.BlockSpec((t, MLA_ROPE), lambda bi, i: (i, 0))
    return pl.pallas_call(
        functools.partial(_mla_proj_kernel, scale=(MLA_NOPE + MLA_ROPE) ** -0.5 * LOG2E),
        grid=(b, s // t),
        in_specs=[_tok_spec(d), _full(g2), _full(w_in_ext), _full(gq2), _full(gkv2), _full(wuqt),
                  _full(wk), _full(wvt), _full(e), tab_t, tab_t, tab, tab],
        out_specs=[_tile_major_spec(hk), _tok_spec(hk), _tile_major_spec(hv)],
        out_shape=[_tile_major_shape(b, s, hk, BF16), jax.ShapeDtypeStruct((b, s, hk), BF16),
                   _tile_major_shape(b, s, hv, BF16)],
        compiler_params=_cparams("parallel", "parallel"),
        name="mla_proj",
    )(h3, g2, w_in_ext, gq2, gkv2, wuqt, wk, wvt, e, cqt, sqt, ck, sk)


def _mla_mixer(h, g, w_in, q_norm, w_uq, kv_norm, w_ukv, b, s):
    n, d = h.shape
    a1 = MLA_Q_RANK + MLA_KV_RANK
    pad = LANES - MLA_NOPE - MLA_ROPE
    swap = _swap_perm(MLA_ROPE, MLA_ROPE, 1)
    bf = lambda w: w.astype(BF16)
    in_cols = -(-(a1 + 2 * MLA_ROPE) // LANES) * LANES
    w_in_ext = bf(jnp.concatenate(
        [w_in, w_in[:, a1:][:, swap], jnp.zeros((d, in_cols - a1 - 2 * MLA_ROPE), w_in.dtype)], axis=1))
    wq = w_uq.reshape(MLA_Q_RANK, N_HEADS, MLA_NOPE + MLA_ROPE)
    zq = lambda w: jnp.zeros((MLA_Q_RANK, N_HEADS, w), w_uq.dtype)
    wuqt = bf(jnp.concatenate([wq, zq(pad)], axis=2).reshape(MLA_Q_RANK, -1).T)
    wkv = w_ukv.reshape(MLA_KV_RANK, N_HEADS, MLA_NOPE + MLA_V)
    wk = bf(jnp.concatenate([wkv[:, :, :MLA_NOPE], jnp.zeros((MLA_KV_RANK, N_HEADS, LANES - MLA_NOPE), w_ukv.dtype)],
                            axis=2).reshape(MLA_KV_RANK, -1))
    wvt = bf(wkv[:, :, MLA_NOPE:].reshape(MLA_KV_RANK, -1).T)
    eye = jnp.broadcast_to(jnp.eye(MLA_ROPE, dtype=BF16)[:, None, :], (MLA_ROPE, N_HEADS, MLA_ROPE))
    ze = lambda w: jnp.zeros((MLA_ROPE, N_HEADS, w), BF16)
    e = jnp.concatenate([ze(MLA_NOPE), eye, ze(pad)], axis=2).reshape(MLA_ROPE, -1)
    ck, sk = _rot_tables(s, MLA_ROPE, 0, MLA_ROPE)
    cq, sq = _rot_tables(s, MLA_ROPE, MLA_NOPE, LANES)
    qt, k, vt = _mla_proj(h.reshape(b, s, d), g, w_in_ext, q_norm, kv_norm, wuqt, wk, wvt, e,
                          cq.T, sq.T, ck, sk)
    return _softmax_attn(qt, k, vt, None, qk_rows=LANES, name="mla_attn").reshape(n, N_HEADS * MLA_V)


def kernel(x, norm_ffn1, ffn1_w_gate_up, ffn1_w_down, norm_mix, norm_ffn2, ffn2_w_gate_up, ffn2_w_down,
           sb_w_qkv, sb_w_o, moba_w_qkv, moba_w_o, mla_w_in, mla_q_norm, mla_w_uq, mla_kv_norm,
           mla_w_ukv, mla_w_o, final_norm):
    b, s, d = x.shape
    h = x.reshape(b * s, d)
    bf = lambda w: w.astype(BF16)
    for i in range(DEPTH):
        h = _ffn(h, None, norm_ffn1[i], bf(ffn1_w_gate_up[i]), bf(ffn1_w_down[i]), final_norm, final=False)
        kind, j = i % 3, i // 3
        if kind == 0:
            mix = (_sb_mixer(h, norm_mix[i], sb_w_qkv[j], b, s), bf(sb_w_o[j]))
        elif kind == 1:
            mix = (_moba_mixer(h, norm_mix[i], moba_w_qkv[j], b, s), bf(moba_w_o[j]))
        else:
            mix = (_mla_mixer(h, norm_mix[i], mla_w_in[j], mla_q_norm[j], mla_w_uq[j], mla_kv_norm[j],
                              mla_w_ukv[j], b, s), bf(mla_w_o[j]))
        h = _ffn(h, mix, norm_ffn2[i], bf(ffn2_w_gate_up[i]), bf(ffn2_w_down[i]), final_norm,
                 final=(i == DEPTH - 1))
    return h.reshape(b, s, d)
```
